```python
import math
import jax, jax.numpy as jnp
from jax import lax
import numpy as np

D_MODEL = 1024
BATCH = 8
SEQ = 4096
DEPTH = 1

D_FF = 2816
D_MIX = D_MODEL
A_WIDTH = D_MIX // 2
A_HEADS = 8
A_HEAD_DIM = A_WIDTH // A_HEADS
CHUNK = 128
B_WIDTH = D_MIX - A_WIDTH
POOL_WINDOWS = (2, 4, 8, 16)
B_GROUPS = len(POOL_WINDOWS)
B_GROUP_DIM = B_WIDTH // B_GROUPS
IN_COLS = 2 * A_WIDTH + B_WIDTH
EPS = 1e-6

kernel_name = "hybrid_gmlp_pool_macaron_block"


def rmsnorm(x, g):
    xf = x.astype(jnp.float32)
    r = lax.rsqrt(jnp.mean(xf * xf, axis=-1, keepdims=True) + EPS)
    return (xf * r).astype(x.dtype) * g


def swiglu(h, w_gate, w_up, w_down):
    return (jax.nn.silu(h @ w_gate) * (h @ w_up)) @ w_down


def gmlp_mixer(u, v, v_gain, w_s, b_s):
    bsz, seq, _ = u.shape
    u = jax.nn.gelu(u)
    v = jax.nn.gelu(v).reshape(bsz, seq, A_HEADS, A_HEAD_DIM)
    v = rmsnorm(v, v_gain.reshape(A_HEADS, A_HEAD_DIM))
    v = v.reshape(bsz, seq // CHUNK, CHUNK, A_HEADS, A_HEAD_DIM)
    causal = jnp.tril(jnp.ones((CHUNK, CHUNK), dtype=bool))
    w = jnp.where(causal[None], w_s, jnp.zeros_like(w_s))
    v = jnp.einsum('hts,bnshd->bnthd', w, v) + b_s.T[None, None, :, :, None]
    return u * v.reshape(bsz, seq, A_WIDTH)


def pool_mixer(z, w_pool, scale):
    bsz, seq, _ = z.shape
    zf = z.astype(jnp.float32)
    cs = jnp.concatenate([jnp.zeros((bsz, 1, B_WIDTH), jnp.float32),
                          jnp.cumsum(zf, axis=1)], axis=1)
    pos = jnp.arange(1, seq + 1, dtype=jnp.int32)
    outs = []
    for g, win in enumerate(POOL_WINDOWS):
        sl = slice(g * B_GROUP_DIM, (g + 1) * B_GROUP_DIM)
        c = cs[..., sl]
        upper = c[:, 1:]
        lower = jnp.concatenate([jnp.zeros((bsz, win - 1, B_GROUP_DIM), jnp.float32),
                                 c[:, :seq - win + 1]], axis=1)
        count = jnp.minimum(pos, win).astype(jnp.float32)[None, :, None]
        d = ((upper - lower) / count - zf[..., sl]).astype(z.dtype)
        outs.append(d @ w_pool[g])
    return jnp.concatenate(outs, axis=-1) * scale


def setup_inputs(seed: int = 0) -> dict:
    key = jax.random.key(seed)
    ks = jax.random.split(key, 20)
    f32 = jnp.float32
    L = DEPTH

    def nrm(k, shape, s):
        return jax.random.normal(k, shape, f32) * s

    def gain(k, shape):
        return 1.0 + 0.05 * jax.random.normal(k, shape, f32)

    return {
        "x": jax.random.normal(ks[0], (BATCH, SEQ, D_MODEL), f32),
        "ffn1_norm": gain(ks[1], (L, D_MODEL)),
        "ffn1_w_gate": nrm(ks[2], (L, D_MODEL, D_FF), D_MODEL ** -0.5),
        "ffn1_w_up": nrm(ks[3], (L, D_MODEL, D_FF), D_MODEL ** -0.5),
        "ffn1_w_down": nrm(ks[4], (L, D_FF, D_MODEL), D_FF ** -0.5),
        "mix_norm": gain(ks[5], (L, D_MODEL)),
        "w_in": nrm(ks[6], (L, D_MODEL, IN_COLS), D_MODEL ** -0.5),
        "gmlp_v_norm": gain(ks[7], (L, A_WIDTH)),
        "gmlp_w_s": nrm(ks[8], (L, A_HEADS, CHUNK, CHUNK), 0.5 * CHUNK ** -0.5),
        "gmlp_b_s": 1.0 + 0.1 * jax.random.normal(ks[9], (L, A_HEADS, CHUNK), f32),
        "pool_w": nrm(ks[10], (L, B_GROUPS, B_GROUP_DIM, B_GROUP_DIM), B_GROUP_DIM ** -0.5),
        "pool_scale": gain(ks[11], (L, B_WIDTH)),
        "w_out": nrm(ks[12], (L, D_MIX, D_MODEL), D_MIX ** -0.5),
        "ffn2_norm": gain(ks[13], (L, D_MODEL)),
        "ffn2_w_gate": nrm(ks[14], (L, D_MODEL, D_FF), D_MODEL ** -0.5),
        "ffn2_w_up": nrm(ks[15], (L, D_MODEL, D_FF), D_MODEL ** -0.5),
        "ffn2_w_down": nrm(ks[16], (L, D_FF, D_MODEL), D_FF ** -0.5),
        "final_norm": gain(ks[17], (D_MODEL,)),
    }


def reference(x, ffn1_norm, ffn1_w_gate, ffn1_w_up, ffn1_w_down, mix_norm, w_in,
              gmlp_v_norm, gmlp_w_s, gmlp_b_s, pool_w, pool_scale, w_out,
              ffn2_norm, ffn2_w_gate, ffn2_w_up, ffn2_w_down, final_norm):
    for l in range(DEPTH):
        x = x + 0.5 * swiglu(rmsnorm(x, ffn1_norm[l]), ffn1_w_gate[l], ffn1_w_up[l], ffn1_w_down[l])
        p = rmsnorm(x, mix_norm[l]) @ w_in[l]
        u_a = p[..., :A_WIDTH]
        v_a = p[..., A_WIDTH:2 * A_WIDTH]
        z_b = p[..., 2 * A_WIDTH:]
        y_a = gmlp_mixer(u_a, v_a, gmlp_v_norm[l], gmlp_w_s[l], gmlp_b_s[l])
        y_b = pool_mixer(z_b, pool_w[l], pool_scale[l])
        x = x + jnp.concatenate([y_a, y_b], axis=-1) @ w_out[l]
        x = x + 0.5 * swiglu(rmsnorm(x, ffn2_norm[l]), ffn2_w_gate[l], ffn2_w_up[l], ffn2_w_down[l])
    return rmsnorm(x, final_norm)
```

```python
import functools

import jax
import jax.numpy as jnp
from jax import lax
from jax.experimental import pallas as pl
from jax.experimental.pallas import tpu as pltpu

EPS = 1e-6
CHUNK = 128
A_HEADS = 8
A_HEAD_DIM = 64
POOL_WINDOWS = (2, 4, 8, 16)
POOL_GROUP_DIM = 128
HALO = 16

V7X_LANES = 128
V7X_MXU_DIM = 256
V7X_VMEM_LIMIT_BYTES = 56 * 1024 * 1024

FFN_TOKENS = 256
MIX_TOKENS = 256
FFN_COLS = 256


def _rmsnorm(x, g):
    r = lax.rsqrt(jnp.mean(x * x, axis=-1, keepdims=True) + EPS)
    return (x * r) * g


def _silu(x):
    return x * (0.5 * jnp.tanh(0.5 * x) + 0.5)


def _gelu_tanh(x):
    c = 0.7978845608028654
    return 0.5 * x * (1.0 + jnp.tanh(c * (x + 0.044715 * (x * x * x))))


def _dot(a, b):
    return jnp.dot(a, b, preferred_element_type=jnp.float32)


def _ffn_kernel(x_ref, g_ref, wg_ref, wu_ref, wd_ref, fg_ref, o_ref, *, final_norm):
    x = x_ref[...]
    h = _rmsnorm(x, g_ref[...]).astype(jnp.bfloat16)
    d_ff = wg_ref.shape[1]
    acts = []
    for c in range(0, d_ff, FFN_COLS):
        gate = _dot(h, wg_ref[:, c:c + FFN_COLS])
        up = _dot(h, wu_ref[:, c:c + FFN_COLS])
        acts.append((_silu(gate) * up).astype(jnp.bfloat16))
    a = jnp.concatenate(acts, axis=1)
    y = x + 0.5 * _dot(a, wd_ref[...])
    if final_norm:
        y = _rmsnorm(y, fg_ref[...])
    o_ref[...] = y


def _ffn_call(x2d, g, wg, wu, wd, fg, *, final_norm):
    n, d = x2d.shape
    d_ff = wg.shape[1]
    assert n % FFN_TOKENS == 0 and d_ff % FFN_COLS == 0
    full = lambda i: (0, 0)
    return pl.pallas_call(
        functools.partial(_ffn_kernel, final_norm=final_norm),
        grid=(n // FFN_TOKENS,),
        in_specs=[
            pl.BlockSpec((FFN_TOKENS, d), lambda i: (i, 0)),
            pl.BlockSpec((1, d), full),
            pl.BlockSpec((d, d_ff), full),
            pl.BlockSpec((d, d_ff), full),
            pl.BlockSpec((d_ff, d), full),
            pl.BlockSpec((1, d), full),
        ],
        out_specs=pl.BlockSpec((FFN_TOKENS, d), lambda i: (i, 0)),
        out_shape=jax.ShapeDtypeStruct((n, d), jnp.float32),
        compiler_params=pltpu.CompilerParams(
            dimension_semantics=("arbitrary",),
            vmem_limit_bytes=V7X_VMEM_LIMIT_BYTES),
        name="ffn_final" if final_norm else "ffn",
    )(x2d, g, wg, wu, wd, fg)


def _mixer_kernel(x_ref, g_ref, win_ref, vg_ref, wcat_ref, bias_ref, wp_ref, ps_ref,
                  wout_ref, o_ref, zprev_ref):
    j = pl.program_id(1)
    tm = x_ref.shape[0]
    a_width = A_HEADS * A_HEAD_DIM
    b_width = len(POOL_WINDOWS) * POOL_GROUP_DIM

    x = x_ref[...]
    h = _rmsnorm(x, g_ref[...]).astype(jnp.bfloat16)
    p = _dot(h, win_ref[...])
    u = _gelu_tanh(p[:, :a_width])
    v = _gelu_tanh(p[:, a_width:2 * a_width])
    z = p[:, 2 * a_width:]

    row_head = lax.broadcasted_iota(jnp.int32, (a_width, a_width), 0) // A_HEAD_DIM
    col_head = lax.broadcasted_iota(jnp.int32, (a_width, a_width), 1) // A_HEAD_DIM
    seg = (row_head == col_head).astype(jnp.bfloat16)
    vv = v * v
    vv_hi = vv.astype(jnp.bfloat16)
    vv_lo = (vv - vv_hi.astype(jnp.float32)).astype(jnp.bfloat16)
    ms = (_dot(vv_hi, seg) + _dot(vv_lo, seg)) * (1.0 / A_HEAD_DIM)
    vn = ((v * lax.rsqrt(ms + EPS)) * vg_ref[...]).astype(jnp.bfloat16)

    wrow = lax.broadcasted_iota(jnp.int32, wcat_ref.shape, 0)
    wcol = lax.broadcasted_iota(jnp.int32, wcat_ref.shape, 1) % CHUNK
    wcat = jnp.where(wcol <= wrow, wcat_ref[...], 0.0).astype(jnp.bfloat16)
    heads_per_group = V7X_MXU_DIM // A_HEAD_DIM
    lane_head = lax.broadcasted_iota(jnp.int32, (CHUNK, V7X_MXU_DIM), 1) // A_HEAD_DIM
    ya_chunks = []
    for c in range(0, tm, CHUNK):
        groups = []
        for gi in range(a_width // V7X_MXU_DIM):
            vg = vn[c:c + CHUNK, gi * V7X_MXU_DIM:(gi + 1) * V7X_MXU_DIM]
            stack = jnp.concatenate(
                [jnp.where(lane_head == hh, vg, jnp.zeros_like(vg))
                 for hh in range(heads_per_group)], axis=0)
            k0 = gi * heads_per_group * CHUNK
            groups.append(_dot(wcat[:, k0:k0 + heads_per_group * CHUNK], stack))
        mixed = jnp.concatenate(groups, axis=1) + bias_ref[...]
        ya_chunks.append(u[c:c + CHUNK, :] * mixed)
    y_a = jnp.concatenate(ya_chunks, axis=0)

    @pl.when(j == 0)
    def _():
        zprev_ref[...] = jnp.zeros_like(zprev_ref)

    zext = jnp.concatenate([zprev_ref[...], z], axis=0)
    zprev_ref[...] = z[tm - HALO:, :]
    pos = j * tm + lax.broadcasted_iota(jnp.int32, (tm, POOL_GROUP_DIM), 0) + 1
    yb_groups = []
    for gi, win in enumerate(POOL_WINDOWS):
        lanes = slice(gi * POOL_GROUP_DIM, (gi + 1) * POOL_GROUP_DIM)
        s = zext[:, lanes]
        step = 1
        while step < win:
            s = s + pltpu.roll(s, step, 0)
            step *= 2
        count = jnp.minimum(pos, win).astype(jnp.float32)
        d = (s[HALO:, :] / count - z[:, lanes]).astype(jnp.bfloat16)
        yb_groups.append(_dot(d, wp_ref[gi]))
    y_b = jnp.concatenate(yb_groups, axis=1) * ps_ref[...]

    y = jnp.concatenate([y_a, y_b], axis=1).astype(jnp.bfloat16)
    o_ref[...] = x + _dot(y, wout_ref[...])


def _mixer_call(x3d, g, w_in, v_gain, wcat, bias, w_pool, p_scale, w_out):
    b, s, d = x3d.shape
    assert s % MIX_TOKENS == 0 and MIX_TOKENS % CHUNK == 0 and MIX_TOKENS >= HALO
    a_width = A_HEADS * A_HEAD_DIM
    b_width = len(POOL_WINDOWS) * POOL_GROUP_DIM
    full2 = lambda bi, j: (0, 0)
    return pl.pallas_call(
        _mixer_kernel,
        grid=(b, s // MIX_TOKENS),
        in_specs=[
            pl.BlockSpec((None, MIX_TOKENS, d), lambda bi, j: (bi, j, 0)),
            pl.BlockSpec((1, d), full2),
            pl.BlockSpec(w_in.shape, full2),
            pl.BlockSpec((1, a_width), full2),
            pl.BlockSpec(wcat.shape, full2),
            pl.BlockSpec(bias.shape, full2),
            pl.BlockSpec(w_pool.shape, lambda bi, j: (0, 0, 0)),
            pl.BlockSpec((1, b_width), full2),
            pl.BlockSpec(w_out.shape, full2),
        ],
        out_specs=pl.BlockSpec((None, MIX_TOKENS, d), lambda bi, j: (bi, j, 0)),
        out_shape=jax.ShapeDtypeStruct((b, s, d), jnp.float32),
        scratch_shapes=[pltpu.VMEM((HALO, b_width), jnp.float32)],
        compiler_params=pltpu.CompilerParams(
            dimension_semantics=("arbitrary", "arbitrary"),
            vmem_limit_bytes=V7X_VMEM_LIMIT_BYTES),
        name="mixer",
    )(x3d, g, w_in, v_gain, wcat, bias, w_pool, p_scale, w_out)


def kernel(x, ffn1_norm, ffn1_w_gate, ffn1_w_up, ffn1_w_down, mix_norm, w_in, gmlp_v_norm,
           gmlp_w_s, gmlp_b_s, pool_w, pool_scale, w_out, ffn2_norm, ffn2_w_gate, ffn2_w_up,
           ffn2_w_down, final_norm):
    bsz, seq, d = x.shape
    depth = ffn1_norm.shape[0]
    bf16 = jnp.bfloat16
    fg = final_norm.reshape(1, d)
    for l in range(depth):
        last = l == depth - 1
        x2d = _ffn_call(x.reshape(bsz * seq, d), ffn1_norm[l].reshape(1, d),
                        ffn1_w_gate[l].astype(bf16), ffn1_w_up[l].astype(bf16),
                        ffn1_w_down[l].astype(bf16), fg, final_norm=False)
        wcat = jnp.transpose(gmlp_w_s[l], (1, 0, 2)).reshape(CHUNK, A_HEADS * CHUNK)
        bias = jnp.repeat(gmlp_b_s[l].T, A_HEAD_DIM, axis=1)
        x3d = _mixer_call(x2d.reshape(bsz, seq, d), mix_norm[l].reshape(1, d),
                          w_in[l].astype(bf16), gmlp_v_norm[l].reshape(1, -1), wcat, bias,
                          pool_w[l].astype(bf16), pool_scale[l].reshape(1, -1),
                          w_out[l].astype(bf16))
        x2d = _ffn_call(x3d.reshape(bsz * seq, d), ffn2_norm[l].reshape(1, d),
                        ffn2_w_gate[l].astype(bf16), ffn2_w_up[l].astype(bf16),
                        ffn2_w_down[l].astype(bf16), fg, final_norm=last)
        x = x2d.reshape(bsz, seq, d)
    return x
```

```python
import functools

import jax
import jax.numpy as jnp
from jax import lax
from jax.experimental import pallas as pl
from jax.experimental.pallas import tpu as pltpu

EPS = 1e-6
CHUNK = 128
A_HEADS = 8
A_HEAD_DIM = 64
POOL_WINDOWS = (2, 4, 8, 16)
POOL_GROUP_DIM = 128
HALO = 16

V7X_LANES = 128
V7X_MXU_DIM = 256
V7X_VMEM_LIMIT_BYTES = 56 * 1024 * 1024

FFN_TOKENS = 1024
MIX_TOKENS = 1024
FFN_COLS = 256


def _rmsnorm(x, g):
    r = lax.rsqrt(jnp.mean(x * x, axis=-1, keepdims=True) + EPS)
    return (x * r) * g


def _silu(x):
    return x * (0.5 * jnp.tanh(0.5 * x) + 0.5)


def _gelu_tanh(x):
    c = 0.7978845608028654
    return 0.5 * x * (1.0 + jnp.tanh(c * (x + 0.044715 * (x * x * x))))


def _dot(a, b):
    return jnp.dot(a, b, preferred_element_type=jnp.float32)


def _ffn_kernel(x_ref, g_ref, wg_ref, wu_ref, wd_ref, fg_ref, o_ref, *, final_norm):
    x = x_ref[...]
    h = _rmsnorm(x, g_ref[...]).astype(jnp.bfloat16)
    d_ff = wg_ref.shape[1]
    acts = []
    for c in range(0, d_ff, FFN_COLS):
        gate = _dot(h, wg_ref[:, c:c + FFN_COLS])
        up = _dot(h, wu_ref[:, c:c + FFN_COLS])
        acts.append((_silu(gate) * up).astype(jnp.bfloat16))
    a = jnp.concatenate(acts, axis=1)
    y = x + 0.5 * _dot(a, wd_ref[...])
    if final_norm:
        y = _rmsnorm(y, fg_ref[...])
    o_ref[...] = y


def _ffn_call(x2d, g, wg, wu, wd, fg, *, final_norm):
    n, d = x2d.shape
    d_ff = wg.shape[1]
    assert n % FFN_TOKENS == 0 and d_ff % FFN_COLS == 0
    full = lambda i: (0, 0)
    return pl.pallas_call(
        functools.partial(_ffn_kernel, final_norm=final_norm),
        grid=(n // FFN_TOKENS,),
        in_specs=[
            pl.BlockSpec((FFN_TOKENS, d), lambda i: (i, 0)),
            pl.BlockSpec((1, d), full),
            pl.BlockSpec((d, d_ff), full),
            pl.BlockSpec((d, d_ff), full),
            pl.BlockSpec((d_ff, d), full),
            pl.BlockSpec((1, d), full),
        ],
        out_specs=pl.BlockSpec((FFN_TOKENS, d), lambda i: (i, 0)),
        out_shape=jax.ShapeDtypeStruct((n, d), jnp.float32),
        compiler_params=pltpu.CompilerParams(
            dimension_semantics=("arbitrary",),
            vmem_limit_bytes=V7X_VMEM_LIMIT_BYTES),
        name="ffn_final" if final_norm else "ffn",
    )(x2d, g, wg, wu, wd, fg)


def _mixer_kernel(x_ref, g_ref, win_ref, vg_ref, wcat_ref, bias_ref, wp_ref, ps_ref,
                  wout_ref, o_ref, zprev_ref):
    j = pl.program_id(1)
    tm = x_ref.shape[0]
    a_width = A_HEADS * A_HEAD_DIM
    b_width = len(POOL_WINDOWS) * POOL_GROUP_DIM

    x = x_ref[...]
    h = _rmsnorm(x, g_ref[...]).astype(jnp.bfloat16)
    p = _dot(h, win_ref[...])
    u = _gelu_tanh(p[:, :a_width])
    v = _gelu_tanh(p[:, a_width:2 * a_width])
    z = p[:, 2 * a_width:]

    row_head = lax.broadcasted_iota(jnp.int32, (a_width, a_width), 0) // A_HEAD_DIM
    col_head = lax.broadcasted_iota(jnp.int32, (a_width, a_width), 1) // A_HEAD_DIM
    seg = (row_head == col_head).astype(jnp.bfloat16)
    vv = v * v
    vv_hi = vv.astype(jnp.bfloat16)
    vv_lo = (vv - vv_hi.astype(jnp.float32)).astype(jnp.bfloat16)
    ms = (_dot(vv_hi, seg) + _dot(vv_lo, seg)) * (1.0 / A_HEAD_DIM)
    vn = ((v * lax.rsqrt(ms + EPS)) * vg_ref[...]).astype(jnp.bfloat16)

    wrow = lax.broadcasted_iota(jnp.int32, wcat_ref.shape, 0)
    wcol = lax.broadcasted_iota(jnp.int32, wcat_ref.shape, 1) % CHUNK
    wcat = jnp.where(wcol <= wrow, wcat_ref[...], 0.0).astype(jnp.bfloat16)
    heads_per_group = V7X_MXU_DIM // A_HEAD_DIM
    lane_head = lax.broadcasted_iota(jnp.int32, (CHUNK, V7X_MXU_DIM), 1) // A_HEAD_DIM
    ya_chunks = []
    for c in range(0, tm, CHUNK):
        groups = []
        for gi in range(a_width // V7X_MXU_DIM):
            vg = vn[c:c + CHUNK, gi * V7X_MXU_DIM:(gi + 1) * V7X_MXU_DIM]
            stack = jnp.concatenate(
                [jnp.where(lane_head == hh, vg, jnp.zeros_like(vg))
                 for hh in range(heads_per_group)], axis=0)
            k0 = gi * heads_per_group * CHUNK
            groups.append(_dot(wcat[:, k0:k0 + heads_per_group * CHUNK], stack))
        mixed = jnp.concatenate(groups, axis=1) + bias_ref[...]
        ya_chunks.append(u[c:c + CHUNK, :] * mixed)
    y_a = jnp.concatenate(ya_chunks, axis=0)

    @pl.when(j == 0)
    def _():
        zprev_ref[...] = jnp.zeros_like(zprev_ref)

    zext = jnp.concatenate([zprev_ref[...], z], axis=0)
    zprev_ref[...] = z[tm - HALO:, :]
    pos = j * tm + lax.broadcasted_iota(jnp.int32, (tm, POOL_GROUP_DIM), 0) + 1
    yb_groups = []
    for gi, win in enumerate(POOL_WINDOWS):
        lanes = slice(gi * POOL_GROUP_DIM, (gi + 1) * POOL_GROUP_DIM)
        s = zext[:, lanes]
        step = 1
        while step < win:
            s = s + pltpu.roll(s, step, 0)
            step *= 2
        count = jnp.minimum(pos, win).astype(jnp.float32)
        d = (s[HALO:, :] / count - z[:, lanes]).astype(jnp.bfloat16)
        yb_groups.append(_dot(d, wp_ref[gi]))
    y_b = jnp.concatenate(yb_groups, axis=1) * ps_ref[...]

    y = jnp.concatenate([y_a, y_b], axis=1).astype(jnp.bfloat16)
    o_ref[...] = x + _dot(y, wout_ref[...])


def _mixer_call(x3d, g, w_in, v_gain, wcat, bias, w_pool, p_scale, w_out):
    b, s, d = x3d.shape
    assert s % MIX_TOKENS == 0 and MIX_TOKENS % CHUNK == 0 and MIX_TOKENS >= HALO
    a_width = A_HEADS * A_HEAD_DIM
    b_width = len(POOL_WINDOWS) * POOL_GROUP_DIM
    full2 = lambda bi, j: (0, 0)
    return pl.pallas_call(
        _mixer_kernel,
        grid=(b, s // MIX_TOKENS),
        in_specs=[
            pl.BlockSpec((None, MIX_TOKENS, d), lambda bi, j: (bi, j, 0)),
            pl.BlockSpec((1, d), full2),
            pl.BlockSpec(w_in.shape, full2),
            pl.BlockSpec((1, a_width), full2),
            pl.BlockSpec(wcat.shape, full2),
            pl.BlockSpec(bias.shape, full2),
            pl.BlockSpec(w_pool.shape, lambda bi, j: (0, 0, 0)),
            pl.BlockSpec((1, b_width), full2),
            pl.BlockSpec(w_out.shape, full2),
        ],
        out_specs=pl.BlockSpec((None, MIX_TOKENS, d), lambda bi, j: (bi, j, 0)),
        out_shape=jax.ShapeDtypeStruct((b, s, d), jnp.float32),
        scratch_shapes=[pltpu.VMEM((HALO, b_width), jnp.float32)],
        compiler_params=pltpu.CompilerParams(
            dimension_semantics=("arbitrary", "arbitrary"),
            vmem_limit_bytes=V7X_VMEM_LIMIT_BYTES),
        name="mixer",
    )(x3d, g, w_in, v_gain, wcat, bias, w_pool, p_scale, w_out)


def kernel(x, ffn1_norm, ffn1_w_gate, ffn1_w_up, ffn1_w_down, mix_norm, w_in, gmlp_v_norm,
           gmlp_w_s, gmlp_b_s, pool_w, pool_scale, w_out, ffn2_norm, ffn2_w_gate, ffn2_w_up,
           ffn2_w_down, final_norm):
    bsz, seq, d = x.shape
    depth = ffn1_norm.shape[0]
    bf16 = jnp.bfloat16
    fg = final_norm.reshape(1, d)
    for l in range(depth):
        last = l == depth - 1
        x2d = _ffn_call(x.reshape(bsz * seq, d), ffn1_norm[l].reshape(1, d),
                        ffn1_w_gate[l].astype(bf16), ffn1_w_up[l].astype(bf16),
                        ffn1_w_down[l].astype(bf16), fg, final_norm=False)
        wcat = jnp.transpose(gmlp_w_s[l], (1, 0, 2)).reshape(CHUNK, A_HEADS * CHUNK)
        bias = jnp.repeat(gmlp_b_s[l].T, A_HEAD_DIM, axis=1)
        x3d = _mixer_call(x2d.reshape(bsz, seq, d), mix_norm[l].reshape(1, d),
                          w_in[l].astype(bf16), gmlp_v_norm[l].reshape(1, -1), wcat, bias,
                          pool_w[l].astype(bf16), pool_scale[l].reshape(1, -1),
                          w_out[l].astype(bf16))
        x2d = _ffn_call(x3d.reshape(bsz * seq, d), ffn2_norm[l].reshape(1, d),
                        ffn2_w_gate[l].astype(bf16), ffn2_w_up[l].astype(bf16),
                        ffn2_w_down[l].astype(bf16), fg, final_norm=last)
        x = x2d.reshape(bsz, seq, d)
    return x
```

```python
import functools

import jax
import jax.numpy as jnp
from jax import lax
from jax.experimental import pallas as pl
from jax.experimental.pallas import tpu as pltpu

EPS = 1e-6
CHUNK = 128
A_HEADS = 8
A_HEAD_DIM = 64
POOL_WINDOWS = (2, 4, 8, 16)
POOL_GROUP_DIM = 128
HALO = 16

V7X_LANES = 128
V7X_MXU_DIM = 256
V7X_VMEM_LIMIT_BYTES = 56 * 1024 * 1024

FFN_TOKENS = 1024
MIX_TOKENS = 1024
FFN_ROWS = 256
MIX_ROWS = 512
FFN_COLS = 256


def _rmsnorm(x, g):
    r = lax.rsqrt(jnp.mean(x * x, axis=-1, keepdims=True) + EPS)
    return (x * r) * g


def _silu(x):
    return x * (0.5 * jnp.tanh(0.5 * x) + 0.5)


def _gelu_tanh(x):
    c = 0.7978845608028654
    return 0.5 * x * (1.0 + jnp.tanh(c * (x + 0.044715 * (x * x * x))))


def _dot(a, b):
    return jnp.dot(a, b, preferred_element_type=jnp.float32)


def _ffn_kernel(x_ref, g_ref, wg_ref, wu_ref, wd_ref, fg_ref, o_ref, *, final_norm):
    d_ff = wg_ref.shape[1]
    for r in range(0, x_ref.shape[0], FFN_ROWS):
        x = x_ref[r:r + FFN_ROWS, :]
        h = _rmsnorm(x, g_ref[...]).astype(jnp.bfloat16)
        acts = []
        for c in range(0, d_ff, FFN_COLS):
            gate = _dot(h, wg_ref[:, c:c + FFN_COLS])
            up = _dot(h, wu_ref[:, c:c + FFN_COLS])
            acts.append((_silu(gate) * up).astype(jnp.bfloat16))
        a = jnp.concatenate(acts, axis=1)
        y = x + 0.5 * _dot(a, wd_ref[...])
        if final_norm:
            y = _rmsnorm(y, fg_ref[...])
        o_ref[r:r + FFN_ROWS, :] = y


def _ffn_call(x2d, g, wg, wu, wd, fg, *, final_norm):
    n, d = x2d.shape
    d_ff = wg.shape[1]
    assert n % FFN_TOKENS == 0 and d_ff % FFN_COLS == 0
    full = lambda i: (0, 0)
    return pl.pallas_call(
        functools.partial(_ffn_kernel, final_norm=final_norm),
        grid=(n // FFN_TOKENS,),
        in_specs=[
            pl.BlockSpec((FFN_TOKENS, d), lambda i: (i, 0)),
            pl.BlockSpec((1, d), full),
            pl.BlockSpec((d, d_ff), full),
            pl.BlockSpec((d, d_ff), full),
            pl.BlockSpec((d_ff, d), full),
            pl.BlockSpec((1, d), full),
        ],
        out_specs=pl.BlockSpec((FFN_TOKENS, d), lambda i: (i, 0)),
        out_shape=jax.ShapeDtypeStruct((n, d), jnp.float32),
        compiler_params=pltpu.CompilerParams(
            dimension_semantics=("arbitrary",),
            vmem_limit_bytes=V7X_VMEM_LIMIT_BYTES),
        name="ffn_final" if final_norm else "ffn",
    )(x2d, g, wg, wu, wd, fg)


def _mixer_kernel(x_ref, g_ref, win_ref, vg_ref, wcat_ref, bias_ref, wp_ref, ps_ref,
                  wout_ref, o_ref, zprev_ref):
    j = pl.program_id(1)
    tm = x_ref.shape[0]
    a_width = A_HEADS * A_HEAD_DIM
    heads_per_group = V7X_MXU_DIM // A_HEAD_DIM
    n_groups = a_width // V7X_MXU_DIM

    row_head = lax.broadcasted_iota(jnp.int32, (V7X_MXU_DIM, V7X_MXU_DIM), 0) // A_HEAD_DIM
    col_head = lax.broadcasted_iota(jnp.int32, (V7X_MXU_DIM, V7X_MXU_DIM), 1) // A_HEAD_DIM
    seg = (row_head == col_head).astype(jnp.bfloat16)
    wrow = lax.broadcasted_iota(jnp.int32, wcat_ref.shape, 0)
    wcol = lax.broadcasted_iota(jnp.int32, wcat_ref.shape, 1) % CHUNK
    wcat = jnp.where(wcol <= wrow, wcat_ref[...], 0.0).astype(jnp.bfloat16)
    lane_head = lax.broadcasted_iota(jnp.int32, (CHUNK, V7X_MXU_DIM), 1) // A_HEAD_DIM

    @pl.when(j == 0)
    def _():
        zprev_ref[...] = jnp.zeros_like(zprev_ref)

    halo = zprev_ref[...]
    for r in range(0, tm, MIX_ROWS):
        x = x_ref[r:r + MIX_ROWS, :]
        h = _rmsnorm(x, g_ref[...]).astype(jnp.bfloat16)
        p = _dot(h, win_ref[...])
        u = _gelu_tanh(p[:, :a_width])
        v = _gelu_tanh(p[:, a_width:2 * a_width])
        z = p[:, 2 * a_width:]

        vv = v * v
        vv_hi = vv.astype(jnp.bfloat16)
        vv_lo = (vv - vv_hi.astype(jnp.float32)).astype(jnp.bfloat16)
        ms = jnp.concatenate(
            [_dot(vv_hi[:, gs:gs + V7X_MXU_DIM], seg) + _dot(vv_lo[:, gs:gs + V7X_MXU_DIM], seg)
             for gs in range(0, a_width, V7X_MXU_DIM)], axis=1) * (1.0 / A_HEAD_DIM)
        vn = ((v * lax.rsqrt(ms + EPS)) * vg_ref[...]).astype(jnp.bfloat16)

        ya_chunks = []
        for c in range(0, MIX_ROWS, CHUNK):
            groups = []
            for gi in range(n_groups):
                vg = vn[c:c + CHUNK, gi * V7X_MXU_DIM:(gi + 1) * V7X_MXU_DIM]
                stack = jnp.concatenate(
                    [jnp.where(lane_head == hh, vg, jnp.zeros_like(vg))
                     for hh in range(heads_per_group)], axis=0)
                k0 = gi * heads_per_group * CHUNK
                groups.append(_dot(wcat[:, k0:k0 + heads_per_group * CHUNK], stack))
            mixed = jnp.concatenate(groups, axis=1) + bias_ref[...]
            ya_chunks.append(u[c:c + CHUNK, :] * mixed)
        y_a = jnp.concatenate(ya_chunks, axis=0)

        zext = jnp.concatenate([halo, z], axis=0)
        halo = z[MIX_ROWS - HALO:, :]
        pos = j * tm + r + lax.broadcasted_iota(jnp.int32, (MIX_ROWS, POOL_GROUP_DIM), 0) + 1
        yb_groups = []
        for gi, win in enumerate(POOL_WINDOWS):
            lanes = slice(gi * POOL_GROUP_DIM, (gi + 1) * POOL_GROUP_DIM)
            s = zext[:, lanes]
            step = 1
            while step < win:
                s = s + pltpu.roll(s, step, 0)
                step *= 2
            count = jnp.minimum(pos, win).astype(jnp.float32)
            d = (s[HALO:, :] / count - z[:, lanes]).astype(jnp.bfloat16)
            yb_groups.append(_dot(d, wp_ref[gi]))
        y_b = jnp.concatenate(yb_groups, axis=1) * ps_ref[...]

        y = jnp.concatenate([y_a, y_b], axis=1).astype(jnp.bfloat16)
        o_ref[r:r + MIX_ROWS, :] = x + _dot(y, wout_ref[...])
    zprev_ref[...] = halo


def _mixer_call(x3d, g, w_in, v_gain, wcat, bias, w_pool, p_scale, w_out):
    b, s, d = x3d.shape
    assert s % MIX_TOKENS == 0 and MIX_TOKENS % MIX_ROWS == 0 and MIX_ROWS % CHUNK == 0
    a_width = A_HEADS * A_HEAD_DIM
    b_width = len(POOL_WINDOWS) * POOL_GROUP_DIM
    full2 = lambda bi, j: (0, 0)
    return pl.pallas_call(
        _mixer_kernel,
        grid=(b, s // MIX_TOKENS),
        in_specs=[
            pl.BlockSpec((None, MIX_TOKENS, d), lambda bi, j: (bi, j, 0)),
            pl.BlockSpec((1, d), full2),
            pl.BlockSpec(w_in.shape, full2),
            pl.BlockSpec((1, a_width), full2),
            pl.BlockSpec(wcat.shape, full2),
            pl.BlockSpec(bias.shape, full2),
            pl.BlockSpec(w_pool.shape, lambda bi, j: (0, 0, 0)),
            pl.BlockSpec((1, b_width), full2),
            pl.BlockSpec(w_out.shape, full2),
        ],
        out_specs=pl.BlockSpec((None, MIX_TOKENS, d), lambda bi, j: (bi, j, 0)),
        out_shape=jax.ShapeDtypeStruct((b, s, d), jnp.float32),
        scratch_shapes=[pltpu.VMEM((HALO, b_width), jnp.float32)],
        compiler_params=pltpu.CompilerParams(
            dimension_semantics=("arbitrary", "arbitrary"),
            vmem_limit_bytes=V7X_VMEM_LIMIT_BYTES),
        name="mixer",
    )(x3d, g, w_in, v_gain, wcat, bias, w_pool, p_scale, w_out)


def kernel(x, ffn1_norm, ffn1_w_gate, ffn1_w_up, ffn1_w_down, mix_norm, w_in, gmlp_v_norm,
           gmlp_w_s, gmlp_b_s, pool_w, pool_scale, w_out, ffn2_norm, ffn2_w_gate, ffn2_w_up,
           ffn2_w_down, final_norm):
    bsz, seq, d = x.shape
    depth = ffn1_norm.shape[0]
    bf16 = jnp.bfloat16
    fg = final_norm.reshape(1, d)
    for l in range(depth):
        last = l == depth - 1
        x2d = _ffn_call(x.reshape(bsz * seq, d), ffn1_norm[l].reshape(1, d),
                        ffn1_w_gate[l].astype(bf16), ffn1_w_up[l].astype(bf16),
                        ffn1_w_down[l].astype(bf16), fg, final_norm=False)
        wcat = jnp.transpose(gmlp_w_s[l], (1, 0, 2)).reshape(CHUNK, A_HEADS * CHUNK)
        bias = jnp.repeat(gmlp_b_s[l].T, A_HEAD_DIM, axis=1)
        x3d = _mixer_call(x2d.reshape(bsz, seq, d), mix_norm[l].reshape(1, d),
                          w_in[l].astype(bf16), gmlp_v_norm[l].reshape(1, -1), wcat, bias,
                          pool_w[l].astype(bf16), pool_scale[l].reshape(1, -1),
                          w_out[l].astype(bf16))
        x2d = _ffn_call(x3d.reshape(bsz * seq, d), ffn2_norm[l].reshape(1, d),
                        ffn2_w_gate[l].astype(bf16), ffn2_w_up[l].astype(bf16),
                        ffn2_w_down[l].astype(bf16), fg, final_norm=last)
        x = x2d.reshape(bsz, seq, d)
    return x
```

```python
import functools

import jax
import jax.numpy as jnp
from jax import lax
from jax.experimental import pallas as pl
from jax.experimental.pallas import tpu as pltpu

EPS = 1e-6
CHUNK = 128
A_HEADS = 8
A_HEAD_DIM = 64
POOL_WINDOWS = (2, 4, 8, 16)
POOL_GROUP_DIM = 128
HALO = 16

V7X_LANES = 128
BF16_SUBLANES = 16
V7X_MXU_DIM = 256
V7X_VMEM_LIMIT_BYTES = 56 * 1024 * 1024

FFN_TOKENS = 1024
MIX_TOKENS = 1024
FFN_ROWS = 256
MIX_ROWS = 512
FFN_COLS = 256
W_PREP_STEPS = 8
M_PREP_STEPS = 4


def _rmsnorm(x, g):
    r = lax.rsqrt(jnp.mean(x * x, axis=-1, keepdims=True) + EPS)
    return (x * r) * g


def _silu(x):
    return x * (0.5 * jnp.tanh(0.5 * x) + 0.5)


def _gelu_tanh(x):
    c = 0.7978845608028654
    return 0.5 * x * (1.0 + jnp.tanh(c * (x + 0.044715 * (x * x * x))))


def _dot(a, b):
    return jnp.dot(a, b, preferred_element_type=jnp.float32)


def _ffn_kernel(x_ref, g_ref, wg_ref, wu_ref, wd_ref, fg_ref, o_ref, wg_s, wu_s, wd_s, *,
                final_norm):
    i = pl.program_id(0)
    d_ff = wg_s.shape[1]

    @pl.when(i < W_PREP_STEPS)
    def _():
        rows = wg_ref.shape[0]
        r0 = pl.multiple_of(i * rows, rows)
        wg_s[pl.ds(r0, rows), :] = wg_ref[...].astype(jnp.bfloat16)
        wu_s[pl.ds(r0, rows), :] = wu_ref[...].astype(jnp.bfloat16)
        rows_d = wd_ref.shape[0]
        r0d = pl.multiple_of(i * rows_d, rows_d)
        wd_s[pl.ds(r0d, rows_d), :] = wd_ref[...].astype(jnp.bfloat16)

    @pl.when(i >= W_PREP_STEPS)
    def _():
        for r in range(0, x_ref.shape[0], FFN_ROWS):
            x = x_ref[r:r + FFN_ROWS, :]
            h = _rmsnorm(x, g_ref[...]).astype(jnp.bfloat16)
            acts = []
            for c in range(0, d_ff, FFN_COLS):
                gate = _dot(h, wg_s[:, c:c + FFN_COLS])
                up = _dot(h, wu_s[:, c:c + FFN_COLS])
                acts.append((_silu(gate) * up).astype(jnp.bfloat16))
            a = jnp.concatenate(acts, axis=1)
            y = x + 0.5 * _dot(a, wd_s[...])
            if final_norm:
                y = _rmsnorm(y, fg_ref[...])
            o_ref[r:r + FFN_ROWS, :] = y


def _ffn_call(x2d, g, wg, wu, wd, fg, *, final_norm):
    n, d = x2d.shape
    d_ff = wg.shape[1]
    assert n % FFN_TOKENS == 0 and d_ff % FFN_COLS == 0
    assert d % (W_PREP_STEPS * BF16_SUBLANES) == 0 and d_ff % (W_PREP_STEPS * BF16_SUBLANES) == 0
    full = lambda i: (0, 0)
    tile = lambda i: (jnp.maximum(i - W_PREP_STEPS, 0), 0)
    chunk = lambda i: (jnp.minimum(i, W_PREP_STEPS - 1), 0)
    return pl.pallas_call(
        functools.partial(_ffn_kernel, final_norm=final_norm),
        grid=(W_PREP_STEPS + n // FFN_TOKENS,),
        in_specs=[
            pl.BlockSpec((FFN_TOKENS, d), tile),
            pl.BlockSpec((1, d), full),
            pl.BlockSpec((d // W_PREP_STEPS, d_ff), chunk),
            pl.BlockSpec((d // W_PREP_STEPS, d_ff), chunk),
            pl.BlockSpec((d_ff // W_PREP_STEPS, d), chunk),
            pl.BlockSpec((1, d), full),
        ],
        out_specs=pl.BlockSpec((FFN_TOKENS, d), tile),
        out_shape=jax.ShapeDtypeStruct((n, d), jnp.float32),
        scratch_shapes=[pltpu.VMEM((d, d_ff), jnp.bfloat16),
                        pltpu.VMEM((d, d_ff), jnp.bfloat16),
                        pltpu.VMEM((d_ff, d), jnp.bfloat16)],
        compiler_params=pltpu.CompilerParams(
            dimension_semantics=("arbitrary",),
            vmem_limit_bytes=V7X_VMEM_LIMIT_BYTES),
        name="ffn_final" if final_norm else "ffn",
    )(x2d, g, wg, wu, wd, fg)


def _mixer_kernel(x_ref, g_ref, win_ref, vg_ref, wcat_ref, bias_ref, wp_ref, ps_ref,
                  wout_ref, o_ref, win_s, wout_s, zprev_ref, *, tiles_per_seq):
    i = pl.program_id(0)

    @pl.when(i < M_PREP_STEPS)
    def _():
        rows = win_ref.shape[0]
        r0 = pl.multiple_of(i * rows, rows)
        win_s[pl.ds(r0, rows), :] = win_ref[...].astype(jnp.bfloat16)
        wout_s[pl.ds(r0, rows), :] = wout_ref[...].astype(jnp.bfloat16)

    @pl.when(i >= M_PREP_STEPS)
    def _():
        j = (i - M_PREP_STEPS) % tiles_per_seq
        _mixer_tile(j, x_ref, g_ref, win_s, vg_ref, wcat_ref, bias_ref, wp_ref, ps_ref, wout_s,
                    o_ref, zprev_ref)


def _mixer_tile(j, x_ref, g_ref, win_ref, vg_ref, wcat_ref, bias_ref, wp_ref, ps_ref,
                wout_ref, o_ref, zprev_ref):
    tm = x_ref.shape[0]
    a_width = A_HEADS * A_HEAD_DIM
    heads_per_group = V7X_MXU_DIM // A_HEAD_DIM
    n_groups = a_width // V7X_MXU_DIM

    row_head = lax.broadcasted_iota(jnp.int32, (V7X_MXU_DIM, V7X_MXU_DIM), 0) // A_HEAD_DIM
    col_head = lax.broadcasted_iota(jnp.int32, (V7X_MXU_DIM, V7X_MXU_DIM), 1) // A_HEAD_DIM
    seg = (row_head == col_head).astype(jnp.bfloat16)
    wrow = lax.broadcasted_iota(jnp.int32, wcat_ref.shape, 0)
    wcol = lax.broadcasted_iota(jnp.int32, wcat_ref.shape, 1) % CHUNK
    wcat = jnp.where(wcol <= wrow, wcat_ref[...], 0.0).astype(jnp.bfloat16)
    lane_head = lax.broadcasted_iota(jnp.int32, (CHUNK, V7X_MXU_DIM), 1) // A_HEAD_DIM

    @pl.when(j == 0)
    def _():
        zprev_ref[...] = jnp.zeros_like(zprev_ref)

    halo = zprev_ref[...]
    for r in range(0, tm, MIX_ROWS):
        x = x_ref[r:r + MIX_ROWS, :]
        h = _rmsnorm(x, g_ref[...]).astype(jnp.bfloat16)
        p = _dot(h, win_ref[...])
        u = _gelu_tanh(p[:, :a_width])
        v = _gelu_tanh(p[:, a_width:2 * a_width])
        z = p[:, 2 * a_width:]

        vv = v * v
        vv_hi = vv.astype(jnp.bfloat16)
        vv_lo = (vv - vv_hi.astype(jnp.float32)).astype(jnp.bfloat16)
        ms = jnp.concatenate(
            [_dot(vv_hi[:, gs:gs + V7X_MXU_DIM], seg) + _dot(vv_lo[:, gs:gs + V7X_MXU_DIM], seg)
             for gs in range(0, a_width, V7X_MXU_DIM)], axis=1) * (1.0 / A_HEAD_DIM)
        vn = ((v * lax.rsqrt(ms + EPS)) * vg_ref[...]).astype(jnp.bfloat16)

        ya_chunks = []
        for c in range(0, MIX_ROWS, CHUNK):
            groups = []
            for gi in range(n_groups):
                vg = vn[c:c + CHUNK, gi * V7X_MXU_DIM:(gi + 1) * V7X_MXU_DIM]
                stack = jnp.concatenate(
                    [jnp.where(lane_head == hh, vg, jnp.zeros_like(vg))
                     for hh in range(heads_per_group)], axis=0)
                k0 = gi * heads_per_group * CHUNK
                groups.append(_dot(wcat[:, k0:k0 + heads_per_group * CHUNK], stack))
            mixed = jnp.concatenate(groups, axis=1) + bias_ref[...]
            ya_chunks.append(u[c:c + CHUNK, :] * mixed)
        y_a = jnp.concatenate(ya_chunks, axis=0)

        zext = jnp.concatenate([halo, z], axis=0)
        halo = z[MIX_ROWS - HALO:, :]
        pos = j * tm + r + lax.broadcasted_iota(jnp.int32, (MIX_ROWS, POOL_GROUP_DIM), 0) + 1
        yb_groups = []
        for gi, win in enumerate(POOL_WINDOWS):
            lanes = slice(gi * POOL_GROUP_DIM, (gi + 1) * POOL_GROUP_DIM)
            s = zext[:, lanes]
            step = 1
            while step < win:
                s = s + pltpu.roll(s, step, 0)
                step *= 2
            count = jnp.minimum(pos, win).astype(jnp.float32)
            d = (s[HALO:, :] / count - z[:, lanes]).astype(jnp.bfloat16)
            yb_groups.append(_dot(d, wp_ref[gi].astype(jnp.bfloat16)))
        y_b = jnp.concatenate(yb_groups, axis=1) * ps_ref[...]

        y = jnp.concatenate([y_a, y_b], axis=1).astype(jnp.bfloat16)
        o_ref[r:r + MIX_ROWS, :] = x + _dot(y, wout_ref[...])
    zprev_ref[...] = halo


def _mixer_call(x3d, g, w_in, v_gain, wcat, bias, w_pool, p_scale, w_out):
    b, s, d = x3d.shape
    assert s % MIX_TOKENS == 0 and MIX_TOKENS % MIX_ROWS == 0 and MIX_ROWS % CHUNK == 0
    a_width = A_HEADS * A_HEAD_DIM
    b_width = len(POOL_WINDOWS) * POOL_GROUP_DIM
    d_in, d_mix = w_in.shape[1], w_out.shape[0]
    assert w_in.shape[0] == d and d_mix == a_width + b_width
    assert d % (M_PREP_STEPS * BF16_SUBLANES) == 0 and d_mix == d
    tiles_per_seq = s // MIX_TOKENS
    full = lambda i: (0, 0)
    chunk = lambda i: (jnp.minimum(i, M_PREP_STEPS - 1), 0)

    def tile(i):
        t = jnp.maximum(i - M_PREP_STEPS, 0)
        return (t // tiles_per_seq, t % tiles_per_seq, 0)

    return pl.pallas_call(
        functools.partial(_mixer_kernel, tiles_per_seq=tiles_per_seq),
        grid=(M_PREP_STEPS + b * tiles_per_seq,),
        in_specs=[
            pl.BlockSpec((None, MIX_TOKENS, d), tile),
            pl.BlockSpec((1, d), full),
            pl.BlockSpec((d // M_PREP_STEPS, d_in), chunk),
            pl.BlockSpec((1, a_width), full),
            pl.BlockSpec(wcat.shape, full),
            pl.BlockSpec(bias.shape, full),
            pl.BlockSpec(w_pool.shape, lambda i: (0, 0, 0)),
            pl.BlockSpec((1, b_width), full),
            pl.BlockSpec((d_mix // M_PREP_STEPS, d), chunk),
        ],
        out_specs=pl.BlockSpec((None, MIX_TOKENS, d), tile),
        out_shape=jax.ShapeDtypeStruct((b, s, d), jnp.float32),
        scratch_shapes=[pltpu.VMEM((d, d_in), jnp.bfloat16),
                        pltpu.VMEM((d_mix, d), jnp.bfloat16),
                        pltpu.VMEM((HALO, b_width), jnp.float32)],
        compiler_params=pltpu.CompilerParams(
            dimension_semantics=("arbitrary",),
            vmem_limit_bytes=V7X_VMEM_LIMIT_BYTES),
        name="mixer",
    )(x3d, g, w_in, v_gain, wcat, bias, w_pool, p_scale, w_out)


def kernel(x, ffn1_norm, ffn1_w_gate, ffn1_w_up, ffn1_w_down, mix_norm, w_in, gmlp_v_norm,
           gmlp_w_s, gmlp_b_s, pool_w, pool_scale, w_out, ffn2_norm, ffn2_w_gate, ffn2_w_up,
           ffn2_w_down, final_norm):
    bsz, seq, d = x.shape
    depth = ffn1_norm.shape[0]
    fg = final_norm.reshape(1, d)
    for l in range(depth):
        last = l == depth - 1
        x2d = _ffn_call(x.reshape(bsz * seq, d), ffn1_norm[l].reshape(1, d),
                        ffn1_w_gate[l], ffn1_w_up[l], ffn1_w_down[l], fg, final_norm=False)
        wcat = jnp.transpose(gmlp_w_s[l], (1, 0, 2)).reshape(CHUNK, A_HEADS * CHUNK)
        bias = jnp.repeat(gmlp_b_s[l].T, A_HEAD_DIM, axis=1)
        x3d = _mixer_call(x2d.reshape(bsz, seq, d), mix_norm[l].reshape(1, d),
                          w_in[l], gmlp_v_norm[l].reshape(1, -1), wcat, bias,
                          pool_w[l], pool_scale[l].reshape(1, -1), w_out[l])
        x2d = _ffn_call(x3d.reshape(bsz * seq, d), ffn2_norm[l].reshape(1, d),
                        ffn2_w_gate[l], ffn2_w_up[l], ffn2_w_down[l], fg, final_norm=last)
        x = x2d.reshape(bsz, seq, d)
    return x
```

```python
import functools

import jax
import jax.numpy as jnp
from jax import lax
from jax.experimental import pallas as pl
from jax.experimental.pallas import tpu as pltpu

EPS = 1e-6
CHUNK = 128
A_HEADS = 8
A_HEAD_DIM = 64
POOL_WINDOWS = (2, 4, 8, 16)
POOL_GROUP_DIM = 128
HALO = 16
assert all(w & (w - 1) == 0 and w <= HALO for w in POOL_WINDOWS)

V7X_MXU_DIM = 256
V7X_VMEM_LIMIT_BYTES = 56 * 1024 * 1024
BF16_SUBLANES = 16

FFN_TOKENS = 1024
MIX_TOKENS = 1024
FFN_ROWS = 256
MIX_ROWS = 512
FFN_COLS = 256
W_PREP_STEPS = 8
M_PREP_STEPS = 4


def _rmsnorm(x, g):
    r = lax.rsqrt(jnp.mean(x * x, axis=-1, keepdims=True) + EPS)
    return (x * r) * g


def _silu(x):
    return x * (0.5 * jnp.tanh(0.5 * x) + 0.5)


def _gelu_tanh(x):
    c = 0.7978845608028654
    return 0.5 * x * (1.0 + jnp.tanh(c * (x + 0.044715 * (x * x * x))))


def _dot(a, b):
    return jnp.dot(a, b, preferred_element_type=jnp.float32)


def _store_bf16_chunk(i, src_ref, dst_ref):
    rows = src_ref.shape[0]
    r0 = pl.multiple_of(i * rows, rows)
    dst_ref[pl.ds(r0, rows), :] = src_ref[...].astype(jnp.bfloat16)


def _ffn_rows(x, g_ref, wg_s, wu_s, wd_s):
    h = _rmsnorm(x, g_ref[...]).astype(jnp.bfloat16)
    acts = []
    for c in range(0, wg_s.shape[1], FFN_COLS):
        gate = _dot(h, wg_s[:, c:c + FFN_COLS])
        up = _dot(h, wu_s[:, c:c + FFN_COLS])
        acts.append((_silu(gate) * up).astype(jnp.bfloat16))
    a = jnp.concatenate(acts, axis=1)
    return x + 0.5 * _dot(a, wd_s[...])


def _mixer_rows(x, halo, first_pos, consts, g_ref, win_s, vg_ref, bias_ref, wp_ref, ps_ref, wout_s):
    seg, wcat, head_masks = consts
    rows = x.shape[0]
    a_width = A_HEADS * A_HEAD_DIM
    heads_per_group = len(head_masks)

    h = _rmsnorm(x, g_ref[...]).astype(jnp.bfloat16)
    p = _dot(h, win_s[...])
    u = _gelu_tanh(p[:, :a_width])
    v = _gelu_tanh(p[:, a_width:2 * a_width])
    z = p[:, 2 * a_width:]

    vv = v * v
    vv_hi = vv.astype(jnp.bfloat16)
    vv_lo = (vv - vv_hi.astype(jnp.float32)).astype(jnp.bfloat16)
    ms = jnp.concatenate(
        [_dot(vv_hi[:, gs:gs + V7X_MXU_DIM], seg) + _dot(vv_lo[:, gs:gs + V7X_MXU_DIM], seg)
         for gs in range(0, a_width, V7X_MXU_DIM)], axis=1) * (1.0 / A_HEAD_DIM)
    vn = ((v * lax.rsqrt(ms + EPS)) * vg_ref[...]).astype(jnp.bfloat16)

    ya_chunks = []
    for c in range(0, rows, CHUNK):
        groups = []
        for gi in range(a_width // V7X_MXU_DIM):
            vg = vn[c:c + CHUNK, gi * V7X_MXU_DIM:(gi + 1) * V7X_MXU_DIM]
            stack = jnp.concatenate([vg * m for m in head_masks], axis=0)
            k0 = gi * heads_per_group * CHUNK
            groups.append(_dot(wcat[:, k0:k0 + heads_per_group * CHUNK], stack))
        mixed = jnp.concatenate(groups, axis=1) + bias_ref[...]
        ya_chunks.append(u[c:c + CHUNK, :] * mixed)
    y_a = jnp.concatenate(ya_chunks, axis=0)

    zext = jnp.concatenate([halo, z], axis=0)
    head_pos = first_pos + lax.broadcasted_iota(jnp.int32, (HALO, POOL_GROUP_DIM), 0)
    yb_groups = []
    for gi, win in enumerate(POOL_WINDOWS):
        lanes = slice(gi * POOL_GROUP_DIM, (gi + 1) * POOL_GROUP_DIM)
        s = zext[:, lanes]
        step = 1
        while step < win:
            s = s + pltpu.roll(s, step, 0)
            step *= 2
        head_count = jnp.minimum(head_pos, win).astype(jnp.float32)
        mean = jnp.concatenate([s[HALO:2 * HALO, :] / head_count,
                                s[2 * HALO:, :] * (1.0 / win)], axis=0)
        d = (mean - z[:, lanes]).astype(jnp.bfloat16)
        yb_groups.append(_dot(d, wp_ref[gi].astype(jnp.bfloat16)))
    y_b = jnp.concatenate(yb_groups, axis=1) * ps_ref[...]

    y = jnp.concatenate([y_a, y_b], axis=1).astype(jnp.bfloat16)
    return x + _dot(y, wout_s[...]), z[rows - HALO:, :]


def _mixer_kernel(x_ref, gm_ref, win_ref, vg_ref, wcat_ref, bias_ref, wp_ref, ps_ref, wout_ref,
                  o_ref, win_s, wout_s, zprev_ref, *, tiles_per_seq):
    i = pl.program_id(0)

    @pl.when(i < M_PREP_STEPS)
    def _():
        _store_bf16_chunk(i, win_ref, win_s)
        _store_bf16_chunk(i, wout_ref, wout_s)

    @pl.when(i >= M_PREP_STEPS)
    def _():
        tm = x_ref.shape[0]
        j = (i - M_PREP_STEPS) % tiles_per_seq

        row_head = lax.broadcasted_iota(jnp.int32, (V7X_MXU_DIM, V7X_MXU_DIM), 0) // A_HEAD_DIM
        col_head = lax.broadcasted_iota(jnp.int32, (V7X_MXU_DIM, V7X_MXU_DIM), 1) // A_HEAD_DIM
        seg = (row_head == col_head).astype(jnp.bfloat16)
        wrow = lax.broadcasted_iota(jnp.int32, wcat_ref.shape, 0)
        wcol = lax.broadcasted_iota(jnp.int32, wcat_ref.shape, 1) % CHUNK
        wcat = jnp.where(wcol <= wrow, wcat_ref[...], 0.0).astype(jnp.bfloat16)
        lane_head = lax.broadcasted_iota(jnp.int32, (CHUNK, V7X_MXU_DIM), 1) // A_HEAD_DIM
        head_masks = [(lane_head == hh).astype(jnp.bfloat16)
                      for hh in range(V7X_MXU_DIM // A_HEAD_DIM)]
        consts = (seg, wcat, head_masks)

        @pl.when(j == 0)
        def _():
            zprev_ref[...] = jnp.zeros_like(zprev_ref)

        halo = zprev_ref[...]
        for r in range(0, tm, MIX_ROWS):
            x2, halo = _mixer_rows(x_ref[r:r + MIX_ROWS, :], halo, j * tm + (r + 1), consts,
                                   gm_ref, win_s, vg_ref, bias_ref, wp_ref, ps_ref, wout_s)
            o_ref[r:r + MIX_ROWS, :] = x2
        zprev_ref[...] = halo


def _mixer_call(x3d, gm, w_in, v_gain, wcat, bias, w_pool, p_scale, w_out):
    b, s, d = x3d.shape
    d_in, d_mix = w_in.shape[1], w_out.shape[0]
    a_width = A_HEADS * A_HEAD_DIM
    b_width = len(POOL_WINDOWS) * POOL_GROUP_DIM
    assert s % MIX_TOKENS == 0 and MIX_TOKENS % MIX_ROWS == 0 and MIX_ROWS % CHUNK == 0
    assert d_in == 2 * a_width + b_width and d_mix == a_width + b_width
    assert d % (M_PREP_STEPS * BF16_SUBLANES) == 0 and d_mix % (M_PREP_STEPS * BF16_SUBLANES) == 0
    tiles_per_seq = s // MIX_TOKENS
    full = lambda i: (0, 0)
    chunk = lambda i: (jnp.minimum(i, M_PREP_STEPS - 1), 0)

    def tile(i):
        t = jnp.maximum(i - M_PREP_STEPS, 0)
        return (t // tiles_per_seq, t % tiles_per_seq, 0)

    return pl.pallas_call(
        functools.partial(_mixer_kernel, tiles_per_seq=tiles_per_seq),
        grid=(M_PREP_STEPS + b * tiles_per_seq,),
        in_specs=[
            pl.BlockSpec((None, MIX_TOKENS, d), tile),
            pl.BlockSpec((1, d), full),
            pl.BlockSpec((d // M_PREP_STEPS, d_in), chunk),
            pl.BlockSpec((1, a_width), full),
            pl.BlockSpec(wcat.shape, full),
            pl.BlockSpec(bias.shape, full),
            pl.BlockSpec(w_pool.shape, lambda i: (0, 0, 0)),
            pl.BlockSpec((1, b_width), full),
            pl.BlockSpec((d_mix // M_PREP_STEPS, d), chunk),
        ],
        out_specs=pl.BlockSpec((None, MIX_TOKENS, d), tile),
        out_shape=jax.ShapeDtypeStruct((b, s, d), jnp.float32),
        scratch_shapes=[pltpu.VMEM((d, d_in), jnp.bfloat16),
                        pltpu.VMEM((d_mix, d), jnp.bfloat16),
                        pltpu.VMEM((HALO, b_width), jnp.float32)],
        compiler_params=pltpu.CompilerParams(
            dimension_semantics=("arbitrary",),
            vmem_limit_bytes=V7X_VMEM_LIMIT_BYTES),
        name="mixer",
    )(x3d, gm, w_in, v_gain, wcat, bias, w_pool, p_scale, w_out)


def _ffn_kernel(x_ref, g_ref, wg_ref, wu_ref, wd_ref, fg_ref, o_ref, wg_s, wu_s, wd_s, *,
                final_norm):
    i = pl.program_id(0)

    @pl.when(i < W_PREP_STEPS)
    def _():
        _store_bf16_chunk(i, wg_ref, wg_s)
        _store_bf16_chunk(i, wu_ref, wu_s)
        _store_bf16_chunk(i, wd_ref, wd_s)

    @pl.when(i >= W_PREP_STEPS)
    def _():
        for r in range(0, x_ref.shape[0], FFN_ROWS):
            y = _ffn_rows(x_ref[r:r + FFN_ROWS, :], g_ref, wg_s, wu_s, wd_s)
            if final_norm:
                y = _rmsnorm(y, fg_ref[...])
            o_ref[r:r + FFN_ROWS, :] = y


def _ffn_call(x2d, g, wg, wu, wd, fg, *, final_norm):
    n, d = x2d.shape
    d_ff = wg.shape[1]
    assert n % FFN_TOKENS == 0 and FFN_TOKENS % FFN_ROWS == 0 and d_ff % FFN_COLS == 0
    assert d % (W_PREP_STEPS * BF16_SUBLANES) == 0 and d_ff % (W_PREP_STEPS * BF16_SUBLANES) == 0
    full = lambda i: (0, 0)
    tile = lambda i: (jnp.maximum(i - W_PREP_STEPS, 0), 0)
    chunk = lambda i: (jnp.minimum(i, W_PREP_STEPS - 1), 0)
    return pl.pallas_call(
        functools.partial(_ffn_kernel, final_norm=final_norm),
        grid=(W_PREP_STEPS + n // FFN_TOKENS,),
        in_specs=[
            pl.BlockSpec((FFN_TOKENS, d), tile),
            pl.BlockSpec((1, d), full),
            pl.BlockSpec((d // W_PREP_STEPS, d_ff), chunk),
            pl.BlockSpec((d // W_PREP_STEPS, d_ff), chunk),
            pl.BlockSpec((d_ff // W_PREP_STEPS, d), chunk),
            pl.BlockSpec((1, d), full),
        ],
        out_specs=pl.BlockSpec((FFN_TOKENS, d), tile),
        out_shape=jax.ShapeDtypeStruct((n, d), jnp.float32),
        scratch_shapes=[pltpu.VMEM((d, d_ff), jnp.bfloat16),
                        pltpu.VMEM((d, d_ff), jnp.bfloat16),
                        pltpu.VMEM((d_ff, d), jnp.bfloat16)],
        compiler_params=pltpu.CompilerParams(
            dimension_semantics=("arbitrary",),
            vmem_limit_bytes=V7X_VMEM_LIMIT_BYTES),
        name="ffn_final" if final_norm else "ffn",
    )(x2d, g, wg, wu, wd, fg)


def kernel(x, ffn1_norm, ffn1_w_gate, ffn1_w_up, ffn1_w_down, mix_norm, w_in, gmlp_v_norm,
           gmlp_w_s, gmlp_b_s, pool_w, pool_scale, w_out, ffn2_norm, ffn2_w_gate, ffn2_w_up,
           ffn2_w_down, final_norm):
    bsz, seq, d = x.shape
    depth = ffn1_norm.shape[0]
    fg = final_norm.reshape(1, d)
    for l in range(depth):
        last = l == depth - 1
        x2d = _ffn_call(x.reshape(bsz * seq, d), ffn1_norm[l].reshape(1, d), ffn1_w_gate[l],
                        ffn1_w_up[l], ffn1_w_down[l], fg, final_norm=False)
        wcat = jnp.transpose(gmlp_w_s[l], (1, 0, 2)).reshape(CHUNK, A_HEADS * CHUNK)
        bias = jnp.repeat(gmlp_b_s[l].T, A_HEAD_DIM, axis=1)
        x3d = _mixer_call(x2d.reshape(bsz, seq, d), mix_norm[l].reshape(1, d), w_in[l],
                          gmlp_v_norm[l].reshape(1, -1), wcat, bias, pool_w[l],
                          pool_scale[l].reshape(1, -1), w_out[l])
        x = _ffn_call(x3d.reshape(bsz * seq, d), ffn2_norm[l].reshape(1, d), ffn2_w_gate[l],
                      ffn2_w_up[l], ffn2_w_down[l], fg, final_norm=last).reshape(bsz, seq, d)
    return x
```

```python
import functools

import jax
import jax.numpy as jnp
from jax import lax
from jax.experimental import pallas as pl
from jax.experimental.pallas import tpu as pltpu

EPS = 1e-6
CHUNK = 128
A_HEADS = 8
A_HEAD_DIM = 64
POOL_WINDOWS = (2, 4, 8, 16)
POOL_GROUP_DIM = 128
HALO = 16
assert all(w & (w - 1) == 0 and w <= HALO for w in POOL_WINDOWS)

V7X_MXU_DIM = 256
V7X_VMEM_LIMIT_BYTES = 56 * 1024 * 1024
BF16_SUBLANES = 16

FFN_TOKENS = 1024
MIX_TOKENS = 1024
FFN_ROWS = 256
MIX_ROWS = 512
FFN_COLS = 256
W_PREP_STEPS = 8
M_PREP_STEPS = 4


def _rmsnorm(x, g):
    r = lax.rsqrt(jnp.mean(x * x, axis=-1, keepdims=True) + EPS)
    return (x * r) * g


def _silu(x):
    return x * (0.5 * jnp.tanh(0.5 * x) + 0.5)


def _gelu_tanh(x):
    c = 0.7978845608028654
    return 0.5 * x * (1.0 + jnp.tanh(c * (x + 0.044715 * (x * x * x))))


def _dot(a, b):
    return jnp.dot(a, b, preferred_element_type=jnp.float32)


def _store_bf16_chunk(i, src_ref, dst_ref):
    rows = src_ref.shape[0]
    r0 = pl.multiple_of(i * rows, rows)
    dst_ref[pl.ds(r0, rows), :] = src_ref[...].astype(jnp.bfloat16)


def _ffn_rows(x, g_ref, wg_s, wu_s, wd_s):
    h = _rmsnorm(x, g_ref[...]).astype(jnp.bfloat16)
    acts = []
    for c in range(0, wg_s.shape[1], FFN_COLS):
        gate = _dot(h, wg_s[:, c:c + FFN_COLS])
        up = _dot(h, wu_s[:, c:c + FFN_COLS])
        acts.append((_silu(gate) * up).astype(jnp.bfloat16))
    a = jnp.concatenate(acts, axis=1)
    return x + 0.5 * _dot(a, wd_s[...])


def _mixer_rows(x, halo, first_pos, consts, g_ref, win_s, vg_ref, bias_ref, wout_s):
    seg, wcat, head_masks = consts
    rows = x.shape[0]
    a_width = A_HEADS * A_HEAD_DIM
    heads_per_group = len(head_masks)

    h = _rmsnorm(x, g_ref[...]).astype(jnp.bfloat16)
    p = _dot(h, win_s[...])
    u = _gelu_tanh(p[:, :a_width])
    v = _gelu_tanh(p[:, a_width:2 * a_width])
    z = p[:, 2 * a_width:]

    vv = v * v
    vv_hi = vv.astype(jnp.bfloat16)
    vv_lo = (vv - vv_hi.astype(jnp.float32)).astype(jnp.bfloat16)
    ms = jnp.concatenate(
        [_dot(vv_hi[:, gs:gs + V7X_MXU_DIM], seg) + _dot(vv_lo[:, gs:gs + V7X_MXU_DIM], seg)
         for gs in range(0, a_width, V7X_MXU_DIM)], axis=1) * (1.0 / A_HEAD_DIM)
    vn = ((v * lax.rsqrt(ms + EPS)) * vg_ref[...]).astype(jnp.bfloat16)

    ya_chunks = []
    for c in range(0, rows, CHUNK):
        groups = []
        for gi in range(a_width // V7X_MXU_DIM):
            vg = vn[c:c + CHUNK, gi * V7X_MXU_DIM:(gi + 1) * V7X_MXU_DIM]
            stack = jnp.concatenate([vg * m for m in head_masks], axis=0)
            k0 = gi * heads_per_group * CHUNK
            groups.append(_dot(wcat[:, k0:k0 + heads_per_group * CHUNK], stack))
        mixed = jnp.concatenate(groups, axis=1) + bias_ref[...]
        ya_chunks.append(u[c:c + CHUNK, :] * mixed)
    y_a = jnp.concatenate(ya_chunks, axis=0)

    zext = jnp.concatenate([halo, z], axis=0)
    head_pos = first_pos + lax.broadcasted_iota(jnp.int32, (HALO, POOL_GROUP_DIM), 0)
    d_groups = []
    for gi, win in enumerate(POOL_WINDOWS):
        lanes = slice(gi * POOL_GROUP_DIM, (gi + 1) * POOL_GROUP_DIM)
        s = zext[:, lanes]
        step = 1
        while step < win:
            s = s + pltpu.roll(s, step, 0)
            step *= 2
        head_count = jnp.minimum(head_pos, win).astype(jnp.float32)
        mean = jnp.concatenate([s[HALO:2 * HALO, :] / head_count,
                                s[2 * HALO:, :] * (1.0 / win)], axis=0)
        d_groups.append((mean - z[:, lanes]).astype(jnp.bfloat16))

    y = jnp.concatenate([y_a.astype(jnp.bfloat16)] + d_groups, axis=1)
    return x + _dot(y, wout_s[...]), z[rows - HALO:, :]


def _mixer_kernel(x_ref, gm_ref, win_ref, vg_ref, wcat_ref, bias_ref, wp_ref, ps_ref, wout_ref,
                  o_ref, win_s, wout_s, zprev_ref, *, tiles_per_seq):
    i = pl.program_id(0)

    @pl.when(i < M_PREP_STEPS)
    def _():
        _store_bf16_chunk(i, win_ref, win_s)

    a_width = A_HEADS * A_HEAD_DIM
    chunk_rows = wout_ref.shape[0]
    for step in range(M_PREP_STEPS):
        @pl.when(i == step)
        def _(step=step):
            r0 = step * chunk_rows
            if r0 < a_width:
                wout_s[r0:r0 + chunk_rows, :] = wout_ref[...].astype(jnp.bfloat16)
            else:
                for lr in range(0, chunk_rows, POOL_GROUP_DIM):
                    g = (r0 + lr - a_width) // POOL_GROUP_DIM
                    lanes = slice(g * POOL_GROUP_DIM, (g + 1) * POOL_GROUP_DIM)
                    folded = jnp.dot(wp_ref[g] * ps_ref[:, lanes], wout_ref[lr:lr + POOL_GROUP_DIM, :],
                                     precision=lax.Precision.HIGHEST,
                                     preferred_element_type=jnp.float32)
                    wout_s[r0 + lr:r0 + lr + POOL_GROUP_DIM, :] = folded.astype(jnp.bfloat16)

    @pl.when(i >= M_PREP_STEPS)
    def _():
        tm = x_ref.shape[0]
        j = (i - M_PREP_STEPS) % tiles_per_seq

        row_head = lax.broadcasted_iota(jnp.int32, (V7X_MXU_DIM, V7X_MXU_DIM), 0) // A_HEAD_DIM
        col_head = lax.broadcasted_iota(jnp.int32, (V7X_MXU_DIM, V7X_MXU_DIM), 1) // A_HEAD_DIM
        seg = (row_head == col_head).astype(jnp.bfloat16)
        wrow = lax.broadcasted_iota(jnp.int32, wcat_ref.shape, 0)
        wcol = lax.broadcasted_iota(jnp.int32, wcat_ref.shape, 1) % CHUNK
        wcat = jnp.where(wcol <= wrow, wcat_ref[...], 0.0).astype(jnp.bfloat16)
        lane_head = lax.broadcasted_iota(jnp.int32, (CHUNK, V7X_MXU_DIM), 1) // A_HEAD_DIM
        head_masks = [(lane_head == hh).astype(jnp.bfloat16)
                      for hh in range(V7X_MXU_DIM // A_HEAD_DIM)]
        consts = (seg, wcat, head_masks)

        @pl.when(j == 0)
        def _():
            zprev_ref[...] = jnp.zeros_like(zprev_ref)

        halo = zprev_ref[...]
        for r in range(0, tm, MIX_ROWS):
            x2, halo = _mixer_rows(x_ref[r:r + MIX_ROWS, :], halo, j * tm + (r + 1), consts,
                                   gm_ref, win_s, vg_ref, bias_ref, wout_s)
            o_ref[r:r + MIX_ROWS, :] = x2
        zprev_ref[...] = halo


def _mixer_call(x3d, gm, w_in, v_gain, wcat, bias, w_pool, p_scale, w_out):
    b, s, d = x3d.shape
    d_in, d_mix = w_in.shape[1], w_out.shape[0]
    a_width = A_HEADS * A_HEAD_DIM
    b_width = len(POOL_WINDOWS) * POOL_GROUP_DIM
    assert s % MIX_TOKENS == 0 and MIX_TOKENS % MIX_ROWS == 0 and MIX_ROWS % CHUNK == 0
    assert d_in == 2 * a_width + b_width and d_mix == a_width + b_width
    assert d % (M_PREP_STEPS * BF16_SUBLANES) == 0 and d_mix % (M_PREP_STEPS * BF16_SUBLANES) == 0
    assert (d_mix // M_PREP_STEPS) % POOL_GROUP_DIM == 0 and a_width % (d_mix // M_PREP_STEPS) == 0
    tiles_per_seq = s // MIX_TOKENS
    full = lambda i: (0, 0)
    chunk = lambda i: (jnp.minimum(i, M_PREP_STEPS - 1), 0)

    def tile(i):
        t = jnp.maximum(i - M_PREP_STEPS, 0)
        return (t // tiles_per_seq, t % tiles_per_seq, 0)

    return pl.pallas_call(
        functools.partial(_mixer_kernel, tiles_per_seq=tiles_per_seq),
        grid=(M_PREP_STEPS + b * tiles_per_seq,),
        in_specs=[
            pl.BlockSpec((None, MIX_TOKENS, d), tile),
            pl.BlockSpec((1, d), full),
            pl.BlockSpec((d // M_PREP_STEPS, d_in), chunk),
            pl.BlockSpec((1, a_width), full),
            pl.BlockSpec(wcat.shape, full),
            pl.BlockSpec(bias.shape, full),
            pl.BlockSpec(w_pool.shape, lambda i: (0, 0, 0)),
            pl.BlockSpec((1, b_width), full),
            pl.BlockSpec((d_mix // M_PREP_STEPS, d), chunk),
        ],
        out_specs=pl.BlockSpec((None, MIX_TOKENS, d), tile),
        out_shape=jax.ShapeDtypeStruct((b, s, d), jnp.float32),
        scratch_shapes=[pltpu.VMEM((d, d_in), jnp.bfloat16),
                        pltpu.VMEM((d_mix, d), jnp.bfloat16),
                        pltpu.VMEM((HALO, b_width), jnp.float32)],
        compiler_params=pltpu.CompilerParams(
            dimension_semantics=("arbitrary",),
            vmem_limit_bytes=V7X_VMEM_LIMIT_BYTES),
        name="mixer",
    )(x3d, gm, w_in, v_gain, wcat, bias, w_pool, p_scale, w_out)


def _ffn_kernel(x_ref, g_ref, wg_ref, wu_ref, wd_ref, fg_ref, o_ref, wg_s, wu_s, wd_s, *,
                final_norm):
    i = pl.program_id(0)

    @pl.when(i < W_PREP_STEPS)
    def _():
        _store_bf16_chunk(i, wg_ref, wg_s)
        _store_bf16_chunk(i, wu_ref, wu_s)
        _store_bf16_chunk(i, wd_ref, wd_s)

    @pl.when(i >= W_PREP_STEPS)
    def _():
        for r in range(0, x_ref.shape[0], FFN_ROWS):
            y = _ffn_rows(x_ref[r:r + FFN_ROWS, :], g_ref, wg_s, wu_s, wd_s)
            if final_norm:
                y = _rmsnorm(y, fg_ref[...])
            o_ref[r:r + FFN_ROWS, :] = y


def _ffn_call(x2d, g, wg, wu, wd, fg, *, final_norm):
    n, d = x2d.shape
    d_ff = wg.shape[1]
    assert n % FFN_TOKENS == 0 and FFN_TOKENS % FFN_ROWS == 0 and d_ff % FFN_COLS == 0
    assert d % (W_PREP_STEPS * BF16_SUBLANES) == 0 and d_ff % (W_PREP_STEPS * BF16_SUBLANES) == 0
    full = lambda i: (0, 0)
    tile = lambda i: (jnp.maximum(i - W_PREP_STEPS, 0), 0)
    chunk = lambda i: (jnp.minimum(i, W_PREP_STEPS - 1), 0)
    return pl.pallas_call(
        functools.partial(_ffn_kernel, final_norm=final_norm),
        grid=(W_PREP_STEPS + n // FFN_TOKENS,),
        in_specs=[
            pl.BlockSpec((FFN_TOKENS, d), tile),
            pl.BlockSpec((1, d), full),
            pl.BlockSpec((d // W_PREP_STEPS, d_ff), chunk),
            pl.BlockSpec((d // W_PREP_STEPS, d_ff), chunk),
            pl.BlockSpec((d_ff // W_PREP_STEPS, d), chunk),
            pl.BlockSpec((1, d), full),
        ],
        out_specs=pl.BlockSpec((FFN_TOKENS, d), tile),
        out_shape=jax.ShapeDtypeStruct((n, d), jnp.float32),
        scratch_shapes=[pltpu.VMEM((d, d_ff), jnp.bfloat16),
                        pltpu.VMEM((d, d_ff), jnp.bfloat16),
                        pltpu.VMEM((d_ff, d), jnp.bfloat16)],
        compiler_params=pltpu.CompilerParams(
            dimension_semantics=("arbitrary",),
            vmem_limit_bytes=V7X_VMEM_LIMIT_BYTES),
        name="ffn_final" if final_norm else "ffn",
    )(x2d, g, wg, wu, wd, fg)


def kernel(x, ffn1_norm, ffn1_w_gate, ffn1_w_up, ffn1_w_down, mix_norm, w_in, gmlp_v_norm,
           gmlp_w_s, gmlp_b_s, pool_w, pool_scale, w_out, ffn2_norm, ffn2_w_gate, ffn2_w_up,
           ffn2_w_down, final_norm):
    bsz, seq, d = x.shape
    depth = ffn1_norm.shape[0]
    fg = final_norm.reshape(1, d)
    for l in range(depth):
        last = l == depth - 1
        x2d = _ffn_call(x.reshape(bsz * seq, d), ffn1_norm[l].reshape(1, d), ffn1_w_gate[l],
                        ffn1_w_up[l], ffn1_w_down[l], fg, final_norm=False)
        wcat = jnp.transpose(gmlp_w_s[l], (1, 0, 2)).reshape(CHUNK, A_HEADS * CHUNK)
        bias = jnp.repeat(gmlp_b_s[l].T, A_HEAD_DIM, axis=1)
        x3d = _mixer_call(x2d.reshape(bsz, seq, d), mix_norm[l].reshape(1, d), w_in[l],
                          gmlp_v_norm[l].reshape(1, -1), wcat, bias, pool_w[l],
                          pool_scale[l].reshape(1, -1), w_out[l])
        x = _ffn_call(x3d.reshape(bsz * seq, d), ffn2_norm[l].reshape(1, d), ffn2_w_gate[l],
                      ffn2_w_up[l], ffn2_w_down[l], fg, final_norm=last).reshape(bsz, seq, d)
    return x
```

```python
import functools

import jax
import jax.numpy as jnp
from jax import lax
from jax.experimental import pallas as pl
from jax.experimental.pallas import tpu as pltpu

EPS = 1e-6
CHUNK = 128
A_HEADS = 8
A_HEAD_DIM = 64
POOL_WINDOWS = (2, 4, 8, 16)
POOL_GROUP_DIM = 128
HALO = 16
assert all(w & (w - 1) == 0 and w <= HALO for w in POOL_WINDOWS)

V7X_MXU_DIM = 256
V7X_VMEM_LIMIT_BYTES = 56 * 1024 * 1024
BF16_SUBLANES = 16

FFN_TOKENS = 1024
MIX_TOKENS = 1024
FFN_ROWS = 256
MIX_ROWS = 512
FFN_COLS = 256
W_PREP_STEPS = 8
M_PREP_STEPS = 4


def _rmsnorm(x, g):
    r = lax.rsqrt(jnp.mean(x * x, axis=-1, keepdims=True) + EPS)
    return (x * r) * g


def _silu(x):
    return x * (0.5 * jnp.tanh(0.5 * x) + 0.5)


def _gelu_tanh(x):
    c = 0.7978845608028654
    return 0.5 * x * (1.0 + jnp.tanh(c * (x + 0.044715 * (x * x * x))))


def _dot(a, b):
    return jnp.dot(a, b, preferred_element_type=jnp.float32)


def _store_bf16_chunk(i, src_ref, dst_ref):
    rows = src_ref.shape[0]
    r0 = pl.multiple_of(i * rows, rows)
    dst_ref[pl.ds(r0, rows), :] = src_ref[...].astype(jnp.bfloat16)


def _ffn_rows(x, g_ref, wg_s, wu_s, wd_s):
    h = _rmsnorm(x, g_ref[...]).astype(jnp.bfloat16)
    acts = []
    for c in range(0, wg_s.shape[1], FFN_COLS):
        gate = _dot(h, wg_s[:, c:c + FFN_COLS])
        up = _dot(h, wu_s[:, c:c + FFN_COLS])
        acts.append((_silu(gate) * up).astype(jnp.bfloat16))
    a = jnp.concatenate(acts, axis=1)
    return x + 0.5 * _dot(a, wd_s[...])


def _mixer_rows(x, halo, first_pos, consts, g_ref, win_s, vg_ref, bias_ref, wout_s):
    seg, wcat, head_masks = consts
    rows = x.shape[0]
    a_width = A_HEADS * A_HEAD_DIM
    heads_per_group = len(head_masks)

    h = _rmsnorm(x, g_ref[...]).astype(jnp.bfloat16)
    p = _dot(h, win_s[...])
    u = _gelu_tanh(p[:, :a_width])
    v = _gelu_tanh(p[:, a_width:2 * a_width])
    z = p[:, 2 * a_width:]

    vv = (v * v).astype(jnp.bfloat16)
    ms = jnp.concatenate(
        [_dot(vv[:, gs:gs + V7X_MXU_DIM], seg) for gs in range(0, a_width, V7X_MXU_DIM)],
        axis=1) * (1.0 / A_HEAD_DIM)
    vn = ((v * lax.rsqrt(ms + EPS)) * vg_ref[...]).astype(jnp.bfloat16)

    ya_chunks = []
    for c in range(0, rows, CHUNK):
        groups = []
        for gi in range(a_width // V7X_MXU_DIM):
            vg = vn[c:c + CHUNK, gi * V7X_MXU_DIM:(gi + 1) * V7X_MXU_DIM]
            stack = jnp.concatenate([vg * m for m in head_masks], axis=0)
            k0 = gi * heads_per_group * CHUNK
            groups.append(_dot(wcat[:, k0:k0 + heads_per_group * CHUNK], stack))
        mixed = jnp.concatenate(groups, axis=1) + bias_ref[...]
        ya_chunks.append(u[c:c + CHUNK, :] * mixed)
    y_a = jnp.concatenate(ya_chunks, axis=0)

    zext = jnp.concatenate([halo, z], axis=0)
    head_pos = first_pos + lax.broadcasted_iota(jnp.int32, (HALO, POOL_GROUP_DIM), 0)
    d_groups = []
    for gi, win in enumerate(POOL_WINDOWS):
        lanes = slice(gi * POOL_GROUP_DIM, (gi + 1) * POOL_GROUP_DIM)
        s = zext[:, lanes]
        step = 1
        while step < win:
            s = s + pltpu.roll(s, step, 0)
            step *= 2
        head_count = jnp.minimum(head_pos, win).astype(jnp.float32)
        mean = jnp.concatenate([s[HALO:2 * HALO, :] / head_count,
                                s[2 * HALO:, :] * (1.0 / win)], axis=0)
        d_groups.append((mean - z[:, lanes]).astype(jnp.bfloat16))

    y = jnp.concatenate([y_a.astype(jnp.bfloat16)] + d_groups, axis=1)
    return x + _dot(y, wout_s[...]), z[rows - HALO:, :]


def _mixer_kernel(x_ref, gm_ref, win_ref, vg_ref, wcat_ref, bias_ref, wp_ref, ps_ref, wout_ref,
                  o_ref, win_s, wout_s, zprev_ref, *, tiles_per_seq):
    i = pl.program_id(0)

    @pl.when(i < M_PREP_STEPS)
    def _():
        _store_bf16_chunk(i, win_ref, win_s)

    a_width = A_HEADS * A_HEAD_DIM
    chunk_rows = wout_ref.shape[0]
    for step in range(M_PREP_STEPS):
        @pl.when(i == step)
        def _(step=step):
            r0 = step * chunk_rows
            if r0 < a_width:
                wout_s[r0:r0 + chunk_rows, :] = wout_ref[...].astype(jnp.bfloat16)
            else:
                for lr in range(0, chunk_rows, POOL_GROUP_DIM):
                    g = (r0 + lr - a_width) // POOL_GROUP_DIM
                    lanes = slice(g * POOL_GROUP_DIM, (g + 1) * POOL_GROUP_DIM)
                    folded = jnp.dot(wp_ref[g] * ps_ref[:, lanes], wout_ref[lr:lr + POOL_GROUP_DIM, :],
                                     precision=lax.Precision.HIGHEST,
                                     preferred_element_type=jnp.float32)
                    wout_s[r0 + lr:r0 + lr + POOL_GROUP_DIM, :] = folded.astype(jnp.bfloat16)

    @pl.when(i >= M_PREP_STEPS)
    def _():
        tm = x_ref.shape[0]
        j = (i - M_PREP_STEPS) % tiles_per_seq

        row_head = lax.broadcasted_iota(jnp.int32, (V7X_MXU_DIM, V7X_MXU_DIM), 0) // A_HEAD_DIM
        col_head = lax.broadcasted_iota(jnp.int32, (V7X_MXU_DIM, V7X_MXU_DIM), 1) // A_HEAD_DIM
        seg = (row_head == col_head).astype(jnp.bfloat16)
        wrow = lax.broadcasted_iota(jnp.int32, wcat_ref.shape, 0)
        wcol = lax.broadcasted_iota(jnp.int32, wcat_ref.shape, 1) % CHUNK
        wcat = jnp.where(wcol <= wrow, wcat_ref[...], 0.0).astype(jnp.bfloat16)
        lane_head = lax.broadcasted_iota(jnp.int32, (CHUNK, V7X_MXU_DIM), 1) // A_HEAD_DIM
        head_masks = [(lane_head == hh).astype(jnp.bfloat16)
                      for hh in range(V7X_MXU_DIM // A_HEAD_DIM)]
        consts = (seg, wcat, head_masks)

        @pl.when(j == 0)
        def _():
            zprev_ref[...] = jnp.zeros_like(zprev_ref)

        halo = zprev_ref[...]
        for r in range(0, tm, MIX_ROWS):
            x2, halo = _mixer_rows(x_ref[r:r + MIX_ROWS, :], halo, j * tm + (r + 1), consts,
                                   gm_ref, win_s, vg_ref, bias_ref, wout_s)
            o_ref[r:r + MIX_ROWS, :] = x2
        zprev_ref[...] = halo


def _mixer_call(x3d, gm, w_in, v_gain, wcat, bias, w_pool, p_scale, w_out):
    b, s, d = x3d.shape
    d_in, d_mix = w_in.shape[1], w_out.shape[0]
    a_width = A_HEADS * A_HEAD_DIM
    b_width = len(POOL_WINDOWS) * POOL_GROUP_DIM
    assert s % MIX_TOKENS == 0 and MIX_TOKENS % MIX_ROWS == 0 and MIX_ROWS % CHUNK == 0
    assert d_in == 2 * a_width + b_width and d_mix == a_width + b_width
    assert d % (M_PREP_STEPS * BF16_SUBLANES) == 0 and d_mix % (M_PREP_STEPS * BF16_SUBLANES) == 0
    assert (d_mix // M_PREP_STEPS) % POOL_GROUP_DIM == 0 and a_width % (d_mix // M_PREP_STEPS) == 0
    tiles_per_seq = s // MIX_TOKENS
    full = lambda i: (0, 0)
    chunk = lambda i: (jnp.minimum(i, M_PREP_STEPS - 1), 0)

    def tile(i):
        t = jnp.maximum(i - M_PREP_STEPS, 0)
        return (t // tiles_per_seq, t % tiles_per_seq, 0)

    return pl.pallas_call(
        functools.partial(_mixer_kernel, tiles_per_seq=tiles_per_seq),
        grid=(M_PREP_STEPS + b * tiles_per_seq,),
        in_specs=[
            pl.BlockSpec((None, MIX_TOKENS, d), tile),
            pl.BlockSpec((1, d), full),
            pl.BlockSpec((d // M_PREP_STEPS, d_in), chunk),
            pl.BlockSpec((1, a_width), full),
            pl.BlockSpec(wcat.shape, full),
            pl.BlockSpec(bias.shape, full),
            pl.BlockSpec(w_pool.shape, lambda i: (0, 0, 0)),
            pl.BlockSpec((1, b_width), full),
            pl.BlockSpec((d_mix // M_PREP_STEPS, d), chunk),
        ],
        out_specs=pl.BlockSpec((None, MIX_TOKENS, d), tile),
        out_shape=jax.ShapeDtypeStruct((b, s, d), jnp.float32),
        scratch_shapes=[pltpu.VMEM((d, d_in), jnp.bfloat16),
                        pltpu.VMEM((d_mix, d), jnp.bfloat16),
                        pltpu.VMEM((HALO, b_width), jnp.float32)],
        compiler_params=pltpu.CompilerParams(
            dimension_semantics=("arbitrary",),
            vmem_limit_bytes=V7X_VMEM_LIMIT_BYTES),
        name="mixer",
    )(x3d, gm, w_in, v_gain, wcat, bias, w_pool, p_scale, w_out)


def _ffn_kernel(x_ref, g_ref, wg_ref, wu_ref, wd_ref, fg_ref, o_ref, wg_s, wu_s, wd_s, *,
                final_norm):
    i = pl.program_id(0)

    @pl.when(i < W_PREP_STEPS)
    def _():
        _store_bf16_chunk(i, wg_ref, wg_s)
        _store_bf16_chunk(i, wu_ref, wu_s)
        _store_bf16_chunk(i, wd_ref, wd_s)

    @pl.when(i >= W_PREP_STEPS)
    def _():
        for r in range(0, x_ref.shape[0], FFN_ROWS):
            y = _ffn_rows(x_ref[r:r + FFN_ROWS, :], g_ref, wg_s, wu_s, wd_s)
            if final_norm:
                y = _rmsnorm(y, fg_ref[...])
            o_ref[r:r + FFN_ROWS, :] = y


def _ffn_call(x2d, g, wg, wu, wd, fg, *, final_norm):
    n, d = x2d.shape
    d_ff = wg.shape[1]
    assert n % FFN_TOKENS == 0 and FFN_TOKENS % FFN_ROWS == 0 and d_ff % FFN_COLS == 0
    assert d % (W_PREP_STEPS * BF16_SUBLANES) == 0 and d_ff % (W_PREP_STEPS * BF16_SUBLANES) == 0
    full = lambda i: (0, 0)
    tile = lambda i: (jnp.maximum(i - W_PREP_STEPS, 0), 0)
    chunk = lambda i: (jnp.minimum(i, W_PREP_STEPS - 1), 0)
    return pl.pallas_call(
        functools.partial(_ffn_kernel, final_norm=final_norm),
        grid=(W_PREP_STEPS + n // FFN_TOKENS,),
        in_specs=[
            pl.BlockSpec((FFN_TOKENS, d), tile),
            pl.BlockSpec((1, d), full),
            pl.BlockSpec((d // W_PREP_STEPS, d_ff), chunk),
            pl.BlockSpec((d // W_PREP_STEPS, d_ff), chunk),
            pl.BlockSpec((d_ff // W_PREP_STEPS, d), chunk),
            pl.BlockSpec((1, d), full),
        ],
        out_specs=pl.BlockSpec((FFN_TOKENS, d), tile),
        out_shape=jax.ShapeDtypeStruct((n, d), jnp.float32),
        scratch_shapes=[pltpu.VMEM((d, d_ff), jnp.bfloat16),
                        pltpu.VMEM((d, d_ff), jnp.bfloat16),
                        pltpu.VMEM((d_ff, d), jnp.bfloat16)],
        compiler_params=pltpu.CompilerParams(
            dimension_semantics=("arbitrary",),
            vmem_limit_bytes=V7X_VMEM_LIMIT_BYTES),
        name="ffn_final" if final_norm else "ffn",
    )(x2d, g, wg, wu, wd, fg)


def kernel(x, ffn1_norm, ffn1_w_gate, ffn1_w_up, ffn1_w_down, mix_norm, w_in, gmlp_v_norm,
           gmlp_w_s, gmlp_b_s, pool_w, pool_scale, w_out, ffn2_norm, ffn2_w_gate, ffn2_w_up,
           ffn2_w_down, final_norm):
    bsz, seq, d = x.shape
    depth = ffn1_norm.shape[0]
    fg = final_norm.reshape(1, d)
    for l in range(depth):
        last = l == depth - 1
        x2d = _ffn_call(x.reshape(bsz * seq, d), ffn1_norm[l].reshape(1, d), ffn1_w_gate[l],
                        ffn1_w_up[l], ffn1_w_down[l], fg, final_norm=False)
        wcat = jnp.transpose(gmlp_w_s[l], (1, 0, 2)).reshape(CHUNK, A_HEADS * CHUNK)
        bias = jnp.repeat(gmlp_b_s[l].T, A_HEAD_DIM, axis=1)
        x3d = _mixer_call(x2d.reshape(bsz, seq, d), mix_norm[l].reshape(1, d), w_in[l],
                          gmlp_v_norm[l].reshape(1, -1), wcat, bias, pool_w[l],
                          pool_scale[l].reshape(1, -1), w_out[l])
        x = _ffn_call(x3d.reshape(bsz * seq, d), ffn2_norm[l].reshape(1, d), ffn2_w_gate[l],
                      ffn2_w_up[l], ffn2_w_down[l], fg, final_norm=last).reshape(bsz, seq, d)
    return x
```

```python
import functools

import jax
import jax.numpy as jnp
from jax import lax
from jax.experimental import pallas as pl
from jax.experimental.pallas import tpu as pltpu

EPS = 1e-6
CHUNK = 128
A_HEADS = 8
A_HEAD_DIM = 64
POOL_WINDOWS = (2, 4, 8, 16)
POOL_GROUP_DIM = 128
HALO = 16
assert all(w & (w - 1) == 0 and w <= HALO for w in POOL_WINDOWS)

V7X_MXU_DIM = 256
V7X_VMEM_LIMIT_BYTES = 56 * 1024 * 1024
BF16_SUBLANES = 16

FFN_TOKENS = 1024
MIX_TOKENS = 1024
FFN_ROWS = 256
MIX_ROWS = 512
FFN_COLS = 256
W_PREP_STEPS = 4
M_PREP_STEPS = 2


def _rmsnorm(x, g):
    r = lax.rsqrt(jnp.mean(x * x, axis=-1, keepdims=True) + EPS)
    return (x * r) * g


def _silu(x):
    return x * (0.5 * jnp.tanh(0.5 * x) + 0.5)


def _gelu_tanh(x):
    c = 0.7978845608028654
    return 0.5 * x * (1.0 + jnp.tanh(c * (x + 0.044715 * (x * x * x))))


def _dot(a, b):
    return jnp.dot(a, b, preferred_element_type=jnp.float32)


def _store_bf16_chunk(i, src_ref, dst_ref):
    rows = src_ref.shape[0]
    r0 = pl.multiple_of(i * rows, rows)
    dst_ref[pl.ds(r0, rows), :] = src_ref[...].astype(jnp.bfloat16)


def _ffn_rows(x, g_ref, wg_s, wu_s, wd_s):
    h = _rmsnorm(x, g_ref[...]).astype(jnp.bfloat16)
    acts = []
    for c in range(0, wg_s.shape[1], FFN_COLS):
        gate = _dot(h, wg_s[:, c:c + FFN_COLS])
        up = _dot(h, wu_s[:, c:c + FFN_COLS])
        acts.append((_silu(gate) * up).astype(jnp.bfloat16))
    a = jnp.concatenate(acts, axis=1)
    return x + 0.5 * _dot(a, wd_s[...])


def _mixer_rows(x, halo, first_pos, consts, g_ref, win_s, vg_ref, bias_ref, wout_s):
    seg, wcat, head_masks = consts
    rows = x.shape[0]
    a_width = A_HEADS * A_HEAD_DIM
    heads_per_group = len(head_masks)

    h = _rmsnorm(x, g_ref[...]).astype(jnp.bfloat16)
    p = _dot(h, win_s[...])
    u = _gelu_tanh(p[:, :a_width])
    v = _gelu_tanh(p[:, a_width:2 * a_width])
    z = p[:, 2 * a_width:]

    vv = (v * v).astype(jnp.bfloat16)
    ms = jnp.concatenate(
        [_dot(vv[:, gs:gs + V7X_MXU_DIM], seg) for gs in range(0, a_width, V7X_MXU_DIM)],
        axis=1) * (1.0 / A_HEAD_DIM)
    vn = ((v * lax.rsqrt(ms + EPS)) * vg_ref[...]).astype(jnp.bfloat16)

    ya_chunks = []
    for c in range(0, rows, CHUNK):
        groups = []
        for gi in range(a_width // V7X_MXU_DIM):
            vg = vn[c:c + CHUNK, gi * V7X_MXU_DIM:(gi + 1) * V7X_MXU_DIM]
            stack = jnp.concatenate([vg * m for m in head_masks], axis=0)
            k0 = gi * heads_per_group * CHUNK
            groups.append(_dot(wcat[:, k0:k0 + heads_per_group * CHUNK], stack))
        mixed = jnp.concatenate(groups, axis=1) + bias_ref[...]
        ya_chunks.append(u[c:c + CHUNK, :] * mixed)
    y_a = jnp.concatenate(ya_chunks, axis=0)

    zext = jnp.concatenate([halo, z], axis=0)
    head_pos = first_pos + lax.broadcasted_iota(jnp.int32, (HALO, POOL_GROUP_DIM), 0)
    d_groups = []
    for gi, win in enumerate(POOL_WINDOWS):
        lanes = slice(gi * POOL_GROUP_DIM, (gi + 1) * POOL_GROUP_DIM)
        s = zext[:, lanes]
        step = 1
        while step < win:
            s = s + pltpu.roll(s, step, 0)
            step *= 2
        head_count = jnp.minimum(head_pos, win).astype(jnp.float32)
        mean = jnp.concatenate([s[HALO:2 * HALO, :] / head_count,
                                s[2 * HALO:, :] * (1.0 / win)], axis=0)
        d_groups.append((mean - z[:, lanes]).astype(jnp.bfloat16))

    y = jnp.concatenate([y_a.astype(jnp.bfloat16)] + d_groups, axis=1)
    return x + _dot(y, wout_s[...]), z[rows - HALO:, :]


def _prep_wout(i, n_steps, wout_ref, wp_ref, ps_ref, wout_s):
    a_width = A_HEADS * A_HEAD_DIM
    chunk_rows = wout_ref.shape[0]
    for step in range(n_steps):
        @pl.when(i == step)
        def _(step=step):
            r0 = step * chunk_rows
            if r0 < a_width:
                wout_s[r0:r0 + chunk_rows, :] = wout_ref[...].astype(jnp.bfloat16)
            else:
                for lr in range(0, chunk_rows, POOL_GROUP_DIM):
                    g = (r0 + lr - a_width) // POOL_GROUP_DIM
                    lanes = slice(g * POOL_GROUP_DIM, (g + 1) * POOL_GROUP_DIM)
                    folded = jnp.dot(wp_ref[g] * ps_ref[:, lanes], wout_ref[lr:lr + POOL_GROUP_DIM, :],
                                     precision=lax.Precision.HIGHEST,
                                     preferred_element_type=jnp.float32)
                    wout_s[r0 + lr:r0 + lr + POOL_GROUP_DIM, :] = folded.astype(jnp.bfloat16)


def _mixer_consts(wcat_ref):
    row_head = lax.broadcasted_iota(jnp.int32, (V7X_MXU_DIM, V7X_MXU_DIM), 0) // A_HEAD_DIM
    col_head = lax.broadcasted_iota(jnp.int32, (V7X_MXU_DIM, V7X_MXU_DIM), 1) // A_HEAD_DIM
    seg = (row_head == col_head).astype(jnp.bfloat16)
    wrow = lax.broadcasted_iota(jnp.int32, wcat_ref.shape, 0)
    wcol = lax.broadcasted_iota(jnp.int32, wcat_ref.shape, 1) % CHUNK
    wcat = jnp.where(wcol <= wrow, wcat_ref[...], 0.0).astype(jnp.bfloat16)
    lane_head = lax.broadcasted_iota(jnp.int32, (CHUNK, V7X_MXU_DIM), 1) // A_HEAD_DIM
    head_masks = [(lane_head == hh).astype(jnp.bfloat16) for hh in range(V7X_MXU_DIM // A_HEAD_DIM)]
    return seg, wcat, head_masks


def _mixer_kernel(x_ref, gm_ref, win_ref, vg_ref, wcat_ref, bias_ref, wp_ref, ps_ref, wout_ref,
                  o_ref, win_s, wout_s, zprev_ref, *, tiles_per_seq):
    i = pl.program_id(0)

    @pl.when(i < M_PREP_STEPS)
    def _():
        _store_bf16_chunk(i, win_ref, win_s)

    _prep_wout(i, M_PREP_STEPS, wout_ref, wp_ref, ps_ref, wout_s)

    @pl.when(i >= M_PREP_STEPS)
    def _():
        tm = x_ref.shape[0]
        j = (i - M_PREP_STEPS) % tiles_per_seq
        consts = _mixer_consts(wcat_ref)

        @pl.when(j == 0)
        def _():
            zprev_ref[...] = jnp.zeros_like(zprev_ref)

        halo = zprev_ref[...]
        for r in range(0, tm, MIX_ROWS):
            x2, halo = _mixer_rows(x_ref[r:r + MIX_ROWS, :], halo, j * tm + (r + 1), consts,
                                   gm_ref, win_s, vg_ref, bias_ref, wout_s)
            o_ref[r:r + MIX_ROWS, :] = x2
        zprev_ref[...] = halo


def _mixer_call(x3d, gm, w_in, v_gain, wcat, bias, w_pool, p_scale, w_out):
    b, s, d = x3d.shape
    d_in, d_mix = w_in.shape[1], w_out.shape[0]
    a_width = A_HEADS * A_HEAD_DIM
    b_width = len(POOL_WINDOWS) * POOL_GROUP_DIM
    assert s % MIX_TOKENS == 0 and MIX_TOKENS % MIX_ROWS == 0 and MIX_ROWS % CHUNK == 0
    assert d_in == 2 * a_width + b_width and d_mix == a_width + b_width
    assert d % (M_PREP_STEPS * BF16_SUBLANES) == 0 and d_mix % (M_PREP_STEPS * BF16_SUBLANES) == 0
    assert (d_mix // M_PREP_STEPS) % POOL_GROUP_DIM == 0 and a_width % (d_mix // M_PREP_STEPS) == 0
    tiles_per_seq = s // MIX_TOKENS
    full = lambda i: (0, 0)
    chunk = lambda i: (jnp.minimum(i, M_PREP_STEPS - 1), 0)

    def tile(i):
        t = jnp.maximum(i - M_PREP_STEPS, 0)
        return (t // tiles_per_seq, t % tiles_per_seq, 0)

    return pl.pallas_call(
        functools.partial(_mixer_kernel, tiles_per_seq=tiles_per_seq),
        grid=(M_PREP_STEPS + b * tiles_per_seq,),
        in_specs=[
            pl.BlockSpec((None, MIX_TOKENS, d), tile),
            pl.BlockSpec((1, d), full),
            pl.BlockSpec((d // M_PREP_STEPS, d_in), chunk),
            pl.BlockSpec((1, a_width), full),
            pl.BlockSpec(wcat.shape, full),
            pl.BlockSpec(bias.shape, full),
            pl.BlockSpec(w_pool.shape, lambda i: (0, 0, 0)),
            pl.BlockSpec((1, b_width), full),
            pl.BlockSpec((d_mix // M_PREP_STEPS, d), chunk),
        ],
        out_specs=pl.BlockSpec((None, MIX_TOKENS, d), tile),
        out_shape=jax.ShapeDtypeStruct((b, s, d), jnp.float32),
        scratch_shapes=[pltpu.VMEM((d, d_in), jnp.bfloat16),
                        pltpu.VMEM((d_mix, d), jnp.bfloat16),
                        pltpu.VMEM((HALO, b_width), jnp.float32)],
        compiler_params=pltpu.CompilerParams(
            dimension_semantics=("arbitrary",),
            vmem_limit_bytes=V7X_VMEM_LIMIT_BYTES),
        name="mixer",
    )(x3d, gm, w_in, v_gain, wcat, bias, w_pool, p_scale, w_out)


def _ffn_kernel(x_ref, g_ref, wg_ref, wu_ref, wd_ref, fg_ref, o_ref, wg_s, wu_s, wd_s, *,
                final_norm):
    i = pl.program_id(0)

    @pl.when(i < W_PREP_STEPS)
    def _():
        _store_bf16_chunk(i, wg_ref, wg_s)
        _store_bf16_chunk(i, wu_ref, wu_s)
        _store_bf16_chunk(i, wd_ref, wd_s)

    @pl.when(i >= W_PREP_STEPS)
    def _():
        for r in range(0, x_ref.shape[0], FFN_ROWS):
            y = _ffn_rows(x_ref[r:r + FFN_ROWS, :], g_ref, wg_s, wu_s, wd_s)
            if final_norm:
                y = _rmsnorm(y, fg_ref[...])
            o_ref[r:r + FFN_ROWS, :] = y


def _ffn_call(x2d, g, wg, wu, wd, fg, *, final_norm):
    n, d = x2d.shape
    d_ff = wg.shape[1]
    assert n % FFN_TOKENS == 0 and FFN_TOKENS % FFN_ROWS == 0 and d_ff % FFN_COLS == 0
    assert d % (W_PREP_STEPS * BF16_SUBLANES) == 0 and d_ff % (W_PREP_STEPS * BF16_SUBLANES) == 0
    full = lambda i: (0, 0)
    tile = lambda i: (jnp.maximum(i - W_PREP_STEPS, 0), 0)
    chunk = lambda i: (jnp.minimum(i, W_PREP_STEPS - 1), 0)
    return pl.pallas_call(
        functools.partial(_ffn_kernel, final_norm=final_norm),
        grid=(W_PREP_STEPS + n // FFN_TOKENS,),
        in_specs=[
            pl.BlockSpec((FFN_TOKENS, d), tile),
            pl.BlockSpec((1, d), full),
            pl.BlockSpec((d // W_PREP_STEPS, d_ff), chunk),
            pl.BlockSpec((d // W_PREP_STEPS, d_ff), chunk),
            pl.BlockSpec((d_ff // W_PREP_STEPS, d), chunk),
            pl.BlockSpec((1, d), full),
        ],
        out_specs=pl.BlockSpec((FFN_TOKENS, d), tile),
        out_shape=jax.ShapeDtypeStruct((n, d), jnp.float32),
        scratch_shapes=[pltpu.VMEM((d, d_ff), jnp.bfloat16),
                        pltpu.VMEM((d, d_ff), jnp.bfloat16),
                        pltpu.VMEM((d_ff, d), jnp.bfloat16)],
        compiler_params=pltpu.CompilerParams(
            dimension_semantics=("arbitrary",),
            vmem_limit_bytes=V7X_VMEM_LIMIT_BYTES),
        name="ffn_final" if final_norm else "ffn",
    )(x2d, g, wg, wu, wd, fg)


def kernel(x, ffn1_norm, ffn1_w_gate, ffn1_w_up, ffn1_w_down, mix_norm, w_in, gmlp_v_norm,
           gmlp_w_s, gmlp_b_s, pool_w, pool_scale, w_out, ffn2_norm, ffn2_w_gate, ffn2_w_up,
           ffn2_w_down, final_norm):
    bsz, seq, d = x.shape
    depth = ffn1_norm.shape[0]
    fg = final_norm.reshape(1, d)
    for l in range(depth):
        last = l == depth - 1
        x2d = _ffn_call(x.reshape(bsz * seq, d), ffn1_norm[l].reshape(1, d), ffn1_w_gate[l],
                        ffn1_w_up[l], ffn1_w_down[l], fg, final_norm=False)
        wcat = jnp.transpose(gmlp_w_s[l], (1, 0, 2)).reshape(CHUNK, A_HEADS * CHUNK)
        bias = jnp.repeat(gmlp_b_s[l].T, A_HEAD_DIM, axis=1)
        x3d = _mixer_call(x2d.reshape(bsz, seq, d), mix_norm[l].reshape(1, d), w_in[l],
                          gmlp_v_norm[l].reshape(1, -1), wcat, bias, pool_w[l],
                          pool_scale[l].reshape(1, -1), w_out[l])
        x = _ffn_call(x3d.reshape(bsz * seq, d), ffn2_norm[l].reshape(1, d), ffn2_w_gate[l],
                      ffn2_w_up[l], ffn2_w_down[l], fg, final_norm=last).reshape(bsz, seq, d)
    return x
```

```python
import functools

import jax
import jax.numpy as jnp
from jax import lax
from jax.experimental import pallas as pl
from jax.experimental.pallas import tpu as pltpu

EPS = 1e-6
CHUNK = 128
A_HEADS = 8
A_HEAD_DIM = 64
POOL_WINDOWS = (2, 4, 8, 16)
POOL_GROUP_DIM = 128
HALO = 16
assert all(w & (w - 1) == 0 and w <= HALO for w in POOL_WINDOWS)

V7X_MXU_DIM = 256
V7X_VMEM_LIMIT_BYTES = 56 * 1024 * 1024
BF16_SUBLANES = 16

FFN_TOKENS = 1024
MIX_TOKENS = 1024
FFN_ROW_BLOCKS = (256, 256, 256, 256)
MIX_ROWS = 512
FFN_COLS = 256
M_PREP_STEPS = 2


def _rmsnorm(x, g):
    r = lax.rsqrt(jnp.mean(x * x, axis=-1, keepdims=True) + EPS)
    return (x * r) * g


def _silu(x):
    return x * (0.5 * jnp.tanh(0.5 * x) + 0.5)


def _gelu_tanh(x):
    c = 0.7978845608028654
    return 0.5 * x * (1.0 + jnp.tanh(c * (x + 0.044715 * (x * x * x))))


def _dot(a, b):
    return jnp.dot(a, b, preferred_element_type=jnp.float32)


def _store_bf16_chunk(i, src_ref, dst_ref):
    rows = src_ref.shape[0]
    r0 = pl.multiple_of(i * rows, rows)
    dst_ref[pl.ds(r0, rows), :] = src_ref[...].astype(jnp.bfloat16)


def _ffn_rows(x, g_ref, wg_s, wu_s, wd_s):
    h = _rmsnorm(x, g_ref[...]).astype(jnp.bfloat16)
    acts = []
    for c in range(wg_s.shape[0]):
        gate = _dot(h, wg_s[c])
        up = _dot(h, wu_s[c])
        acts.append((_silu(gate) * up).astype(jnp.bfloat16))
    a = jnp.concatenate(acts, axis=1)
    return x + 0.5 * _dot(a, wd_s[...])


def _mixer_rows(x, halo, first_pos, consts, g_ref, win_s, vg_ref, bias_ref, wout_s):
    seg, wcat, head_masks = consts
    rows = x.shape[0]
    a_width = A_HEADS * A_HEAD_DIM
    heads_per_group = len(head_masks)

    h = _rmsnorm(x, g_ref[...]).astype(jnp.bfloat16)
    p = _dot(h, win_s[...])
    u = _gelu_tanh(p[:, :a_width])
    v = _gelu_tanh(p[:, a_width:2 * a_width])
    z = p[:, 2 * a_width:]

    vv = (v * v).astype(jnp.bfloat16)
    ms = jnp.concatenate(
        [_dot(vv[:, gs:gs + V7X_MXU_DIM], seg) for gs in range(0, a_width, V7X_MXU_DIM)],
        axis=1) * (1.0 / A_HEAD_DIM)
    vn = ((v * lax.rsqrt(ms + EPS)) * vg_ref[...]).astype(jnp.bfloat16)

    ya_chunks = []
    for c in range(0, rows, CHUNK):
        groups = []
        for gi in range(a_width // V7X_MXU_DIM):
            vg = vn[c:c + CHUNK, gi * V7X_MXU_DIM:(gi + 1) * V7X_MXU_DIM]
            stack = jnp.concatenate([vg * m for m in head_masks], axis=0)
            k0 = gi * heads_per_group * CHUNK
            groups.append(_dot(wcat[:, k0:k0 + heads_per_group * CHUNK], stack))
        mixed = jnp.concatenate(groups, axis=1) + bias_ref[...]
        ya_chunks.append(u[c:c + CHUNK, :] * mixed)
    y_a = jnp.concatenate(ya_chunks, axis=0)

    zext = jnp.concatenate([halo, z], axis=0)
    head_pos = first_pos + lax.broadcasted_iota(jnp.int32, (HALO, POOL_GROUP_DIM), 0)
    d_groups = []
    for gi, win in enumerate(POOL_WINDOWS):
        lanes = slice(gi * POOL_GROUP_DIM, (gi + 1) * POOL_GROUP_DIM)
        s = zext[:, lanes]
        step = 1
        while step < win:
            s = s + pltpu.roll(s, step, 0)
            step *= 2
        head_count = jnp.minimum(head_pos, win).astype(jnp.float32)
        mean = jnp.concatenate([s[HALO:2 * HALO, :] / head_count,
                                s[2 * HALO:, :] * (1.0 / win)], axis=0)
        d_groups.append((mean - z[:, lanes]).astype(jnp.bfloat16))

    y = jnp.concatenate([y_a.astype(jnp.bfloat16)] + d_groups, axis=1)
    return x + _dot(y, wout_s[...]), z[rows - HALO:, :]


def _prep_wout(i, n_steps, wout_ref, wp_ref, ps_ref, wout_s):
    a_width = A_HEADS * A_HEAD_DIM
    chunk_rows = wout_ref.shape[0]
    for step in range(n_steps):
        @pl.when(i == step)
        def _(step=step):
            r0 = step * chunk_rows
            if r0 < a_width:
                wout_s[r0:r0 + chunk_rows, :] = wout_ref[...].astype(jnp.bfloat16)
            else:
                for lr in range(0, chunk_rows, POOL_GROUP_DIM):
                    g = (r0 + lr - a_width) // POOL_GROUP_DIM
                    lanes = slice(g * POOL_GROUP_DIM, (g + 1) * POOL_GROUP_DIM)
                    folded = jnp.dot(wp_ref[g] * ps_ref[:, lanes], wout_ref[lr:lr + POOL_GROUP_DIM, :],
                                     precision=lax.Precision.HIGHEST,
                                     preferred_element_type=jnp.float32)
                    wout_s[r0 + lr:r0 + lr + POOL_GROUP_DIM, :] = folded.astype(jnp.bfloat16)


def _mixer_consts(wcat_ref):
    row_head = lax.broadcasted_iota(jnp.int32, (V7X_MXU_DIM, V7X_MXU_DIM), 0) // A_HEAD_DIM
    col_head = lax.broadcasted_iota(jnp.int32, (V7X_MXU_DIM, V7X_MXU_DIM), 1) // A_HEAD_DIM
    seg = (row_head == col_head).astype(jnp.bfloat16)
    wrow = lax.broadcasted_iota(jnp.int32, wcat_ref.shape, 0)
    wcol = lax.broadcasted_iota(jnp.int32, wcat_ref.shape, 1) % CHUNK
    wcat = jnp.where(wcol <= wrow, wcat_ref[...], 0.0).astype(jnp.bfloat16)
    lane_head = lax.broadcasted_iota(jnp.int32, (CHUNK, V7X_MXU_DIM), 1) // A_HEAD_DIM
    head_masks = [(lane_head == hh).astype(jnp.bfloat16) for hh in range(V7X_MXU_DIM // A_HEAD_DIM)]
    return seg, wcat, head_masks


def _mixer_kernel(x_ref, gm_ref, win_ref, vg_ref, wcat_ref, bias_ref, wp_ref, ps_ref, wout_ref,
                  o_ref, win_s, wout_s, zprev_ref, *, tiles_per_seq):
    i = pl.program_id(0)

    @pl.when(i < M_PREP_STEPS)
    def _():
        _store_bf16_chunk(i, win_ref, win_s)

    _prep_wout(i, M_PREP_STEPS, wout_ref, wp_ref, ps_ref, wout_s)

    @pl.when(i >= M_PREP_STEPS)
    def _():
        tm = x_ref.shape[0]
        j = (i - M_PREP_STEPS) % tiles_per_seq
        consts = _mixer_consts(wcat_ref)

        @pl.when(j == 0)
        def _():
            zprev_ref[...] = jnp.zeros_like(zprev_ref)

        halo = zprev_ref[...]
        for r in range(0, tm, MIX_ROWS):
            x2, halo = _mixer_rows(x_ref[r:r + MIX_ROWS, :], halo, j * tm + (r + 1), consts,
                                   gm_ref, win_s, vg_ref, bias_ref, wout_s)
            o_ref[r:r + MIX_ROWS, :] = x2
        zprev_ref[...] = halo


def _mixer_call(x3d, gm, w_in, v_gain, wcat, bias, w_pool, p_scale, w_out):
    b, s, d = x3d.shape
    d_in, d_mix = w_in.shape[1], w_out.shape[0]
    a_width = A_HEADS * A_HEAD_DIM
    b_width = len(POOL_WINDOWS) * POOL_GROUP_DIM
    assert s % MIX_TOKENS == 0 and MIX_TOKENS % MIX_ROWS == 0 and MIX_ROWS % CHUNK == 0
    assert d_in == 2 * a_width + b_width and d_mix == a_width + b_width
    assert d % (M_PREP_STEPS * BF16_SUBLANES) == 0 and d_mix % (M_PREP_STEPS * BF16_SUBLANES) == 0
    assert (d_mix // M_PREP_STEPS) % POOL_GROUP_DIM == 0 and a_width % (d_mix // M_PREP_STEPS) == 0
    tiles_per_seq = s // MIX_TOKENS
    full = lambda i: (0, 0)
    chunk = lambda i: (jnp.minimum(i, M_PREP_STEPS - 1), 0)

    def tile(i):
        t = jnp.maximum(i - M_PREP_STEPS, 0)
        return (t // tiles_per_seq, t % tiles_per_seq, 0)

    return pl.pallas_call(
        functools.partial(_mixer_kernel, tiles_per_seq=tiles_per_seq),
        grid=(M_PREP_STEPS + b * tiles_per_seq,),
        in_specs=[
            pl.BlockSpec((None, MIX_TOKENS, d), tile),
            pl.BlockSpec((1, d), full),
            pl.BlockSpec((d // M_PREP_STEPS, d_in), chunk),
            pl.BlockSpec((1, a_width), full),
            pl.BlockSpec(wcat.shape, full),
            pl.BlockSpec(bias.shape, full),
            pl.BlockSpec(w_pool.shape, lambda i: (0, 0, 0)),
            pl.BlockSpec((1, b_width), full),
            pl.BlockSpec((d_mix // M_PREP_STEPS, d), chunk),
        ],
        out_specs=pl.BlockSpec((None, MIX_TOKENS, d), tile),
        out_shape=jax.ShapeDtypeStruct((b, s, d), jnp.float32),
        scratch_shapes=[pltpu.VMEM((d, d_in), jnp.bfloat16),
                        pltpu.VMEM((d_mix, d), jnp.bfloat16),
                        pltpu.VMEM((HALO, b_width), jnp.float32)],
        compiler_params=pltpu.CompilerParams(
            dimension_semantics=("arbitrary",),
            vmem_limit_bytes=V7X_VMEM_LIMIT_BYTES),
        name="mixer",
    )(x3d, gm, w_in, v_gain, wcat, bias, w_pool, p_scale, w_out)


def _ffn_kernel(x_ref, g_ref, wg_ref, wu_ref, wd_ref, fg_ref, o_ref, wg_s, wu_s, wd_s, h_s, acc_s,
                *, final_norm):
    i = pl.program_id(0)
    n_chunks = wg_s.shape[0]

    @pl.when(i == 0)
    def _():
        h_s[...] = _rmsnorm(x_ref[...], g_ref[...]).astype(jnp.bfloat16)
        acc_s[...] = jnp.zeros_like(acc_s)

    @pl.when(i < n_chunks)
    def _():
        wg = wg_ref[...].astype(jnp.bfloat16)
        wu = wu_ref[...].astype(jnp.bfloat16)
        wd = wd_ref[...].astype(jnp.bfloat16)
        wg_s[i] = wg
        wu_s[i] = wu
        _store_bf16_chunk(i, wd_ref, wd_s)
        h = h_s[...]
        a = (_silu(_dot(h, wg)) * _dot(h, wu)).astype(jnp.bfloat16)
        acc_s[...] += _dot(a, wd)

    @pl.when(i == n_chunks - 1)
    def _():
        y = x_ref[...] + 0.5 * acc_s[...]
        if final_norm:
            y = _rmsnorm(y, fg_ref[...])
        o_ref[...] = y

    @pl.when(i >= n_chunks)
    def _():
        r = 0
        for rows in FFN_ROW_BLOCKS:
            y = _ffn_rows(x_ref[r:r + rows, :], g_ref, wg_s, wu_s, wd_s)
            if final_norm:
                y = _rmsnorm(y, fg_ref[...])
            o_ref[r:r + rows, :] = y
            r += rows


def _ffn_call(x2d, g, wg, wu, wd, fg, *, final_norm):
    n, d = x2d.shape
    d_ff = wg.shape[1]
    assert n % FFN_TOKENS == 0 and sum(FFN_ROW_BLOCKS) == FFN_TOKENS and d_ff % FFN_COLS == 0
    n_chunks = d_ff // FFN_COLS
    full = lambda i: (0, 0)
    tile = lambda i: (jnp.maximum(i - (n_chunks - 1), 0), 0)
    col_chunk = lambda i: (0, jnp.minimum(i, n_chunks - 1))
    row_chunk = lambda i: (jnp.minimum(i, n_chunks - 1), 0)
    return pl.pallas_call(
        functools.partial(_ffn_kernel, final_norm=final_norm),
        grid=(n_chunks - 1 + n // FFN_TOKENS,),
        in_specs=[
            pl.BlockSpec((FFN_TOKENS, d), tile),
            pl.BlockSpec((1, d), full),
            pl.BlockSpec((d, FFN_COLS), col_chunk),
            pl.BlockSpec((d, FFN_COLS), col_chunk),
            pl.BlockSpec((FFN_COLS, d), row_chunk),
            pl.BlockSpec((1, d), full),
        ],
        out_specs=pl.BlockSpec((FFN_TOKENS, d), tile),
        out_shape=jax.ShapeDtypeStruct((n, d), jnp.float32),
        scratch_shapes=[pltpu.VMEM((n_chunks, d, FFN_COLS), jnp.bfloat16),
                        pltpu.VMEM((n_chunks, d, FFN_COLS), jnp.bfloat16),
                        pltpu.VMEM((d_ff, d), jnp.bfloat16),
                        pltpu.VMEM((FFN_TOKENS, d), jnp.bfloat16),
                        pltpu.VMEM((FFN_TOKENS, d), jnp.float32)],
        compiler_params=pltpu.CompilerParams(
            dimension_semantics=("arbitrary",),
            vmem_limit_bytes=V7X_VMEM_LIMIT_BYTES),
        name="ffn_final" if final_norm else "ffn",
    )(x2d, g, wg, wu, wd, fg)


def kernel(x, ffn1_norm, ffn1_w_gate, ffn1_w_up, ffn1_w_down, mix_norm, w_in, gmlp_v_norm,
           gmlp_w_s, gmlp_b_s, pool_w, pool_scale, w_out, ffn2_norm, ffn2_w_gate, ffn2_w_up,
           ffn2_w_down, final_norm):
    bsz, seq, d = x.shape
    depth = ffn1_norm.shape[0]
    fg = final_norm.reshape(1, d)
    for l in range(depth):
        last = l == depth - 1
        x2d = _ffn_call(x.reshape(bsz * seq, d), ffn1_norm[l].reshape(1, d), ffn1_w_gate[l],
                        ffn1_w_up[l], ffn1_w_down[l], fg, final_norm=False)
        wcat = jnp.transpose(gmlp_w_s[l], (1, 0, 2)).reshape(CHUNK, A_HEADS * CHUNK)
        bias = jnp.repeat(gmlp_b_s[l].T, A_HEAD_DIM, axis=1)
        x3d = _mixer_call(x2d.reshape(bsz, seq, d), mix_norm[l].reshape(1, d), w_in[l],
                          gmlp_v_norm[l].reshape(1, -1), wcat, bias, pool_w[l],
                          pool_scale[l].reshape(1, -1), w_out[l])
        x = _ffn_call(x3d.reshape(bsz * seq, d), ffn2_norm[l].reshape(1, d), ffn2_w_gate[l],
                      ffn2_w_up[l], ffn2_w_down[l], fg, final_norm=last).reshape(bsz, seq, d)
    return x
```

```python
import functools

import jax
import jax.numpy as jnp
from jax import lax
from jax.experimental import pallas as pl
from jax.experimental.pallas import tpu as pltpu

EPS = 1e-6
CHUNK = 128
A_HEADS = 8
A_HEAD_DIM = 64
POOL_WINDOWS = (2, 4, 8, 16)
POOL_GROUP_DIM = 128
HALO = 16
assert all(w & (w - 1) == 0 and w <= HALO for w in POOL_WINDOWS)

V7X_MXU_DIM = 256
V7X_VMEM_LIMIT_BYTES = 56 * 1024 * 1024
BF16_SUBLANES = 16

FFN_TOKENS = 1024
MIX_TOKENS = 1024
FFN_ROW_BLOCKS = (256, 256, 256, 256)
MIX_ROWS = 512
FFN_COLS = 256
M_PREP_STEPS = 2


def _rmsnorm(x, g):
    r = lax.rsqrt(jnp.mean(x * x, axis=-1, keepdims=True) + EPS)
    return (x * r) * g


def _silu(x):
    return x * (0.5 * jnp.tanh(0.5 * x) + 0.5)


def _gelu_tanh(x):
    c = 0.7978845608028654
    return 0.5 * x * (1.0 + jnp.tanh(c * (x + 0.044715 * (x * x * x))))


def _dot(a, b):
    return jnp.dot(a, b, preferred_element_type=jnp.float32)


def _store_bf16_chunk(i, src_ref, dst_ref):
    rows = src_ref.shape[0]
    r0 = pl.multiple_of(i * rows, rows)
    dst_ref[pl.ds(r0, rows), :] = src_ref[...].astype(jnp.bfloat16)


def _ffn_rows(x, g_ref, wg_s, wu_s, wd_s):
    h = _rmsnorm(x, g_ref[...]).astype(jnp.bfloat16)
    acts = []
    for c in range(wg_s.shape[0]):
        gate = _dot(h, wg_s[c])
        up = _dot(h, wu_s[c])
        acts.append((_silu(gate) * up).astype(jnp.bfloat16))
    a = jnp.concatenate(acts, axis=1)
    return x + 0.5 * _dot(a, wd_s[...])


def _mixer_rows(x, halo, first_pos, consts, g_ref, win_s, vg_ref, bias_ref, wout_s):
    seg, wcat, head_masks = consts
    rows = x.shape[0]
    a_width = A_HEADS * A_HEAD_DIM
    heads_per_group = len(head_masks)

    h = _rmsnorm(x, g_ref[...]).astype(jnp.bfloat16)
    p = _dot(h, win_s[...])
    u = _gelu_tanh(p[:, :a_width])
    v = _gelu_tanh(p[:, a_width:2 * a_width])
    z = p[:, 2 * a_width:]

    vv = (v * v).astype(jnp.bfloat16)
    ms = jnp.concatenate(
        [_dot(vv[:, gs:gs + V7X_MXU_DIM], seg) for gs in range(0, a_width, V7X_MXU_DIM)],
        axis=1) * (1.0 / A_HEAD_DIM)
    vn = ((v * lax.rsqrt(ms + EPS)) * vg_ref[...]).astype(jnp.bfloat16)

    ya_chunks = []
    for c in range(0, rows, CHUNK):
        groups = []
        for gi in range(a_width // V7X_MXU_DIM):
            vg = vn[c:c + CHUNK, gi * V7X_MXU_DIM:(gi + 1) * V7X_MXU_DIM]
            stack = jnp.concatenate([vg * m for m in head_masks], axis=0)
            k0 = gi * heads_per_group * CHUNK
            groups.append(_dot(wcat[:, k0:k0 + heads_per_group * CHUNK], stack))
        mixed = jnp.concatenate(groups, axis=1) + bias_ref[...]
        ya_chunks.append(u[c:c + CHUNK, :] * mixed)
    y_a = jnp.concatenate(ya_chunks, axis=0)

    zext = jnp.concatenate([halo, z], axis=0)
    head_pos = first_pos + lax.broadcasted_iota(jnp.int32, (HALO, POOL_GROUP_DIM), 0)
    d_groups = []
    for gi, win in enumerate(POOL_WINDOWS):
        lanes = slice(gi * POOL_GROUP_DIM, (gi + 1) * POOL_GROUP_DIM)
        s = zext[:, lanes]
        step = 1
        while step < win:
            s = s + pltpu.roll(s, step, 0)
            step *= 2
        head_count = jnp.minimum(head_pos, win).astype(jnp.float32)
        mean = jnp.concatenate([s[HALO:2 * HALO, :] / head_count,
                                s[2 * HALO:, :] * (1.0 / win)], axis=0)
        d_groups.append((mean - z[:, lanes]).astype(jnp.bfloat16))

    y = jnp.concatenate([y_a.astype(jnp.bfloat16)] + d_groups, axis=1)
    return x + _dot(y, wout_s[...]), z[rows - HALO:, :]


def _prep_wout(i, n_steps, wout_ref, wp_ref, ps_ref, wout_s):
    a_width = A_HEADS * A_HEAD_DIM
    chunk_rows = wout_ref.shape[0]
    for step in range(n_steps):
        @pl.when(i == step)
        def _(step=step):
            r0 = step * chunk_rows
            if r0 < a_width:
                wout_s[r0:r0 + chunk_rows, :] = wout_ref[...].astype(jnp.bfloat16)
            else:
                for lr in range(0, chunk_rows, POOL_GROUP_DIM):
                    g = (r0 + lr - a_width) // POOL_GROUP_DIM
                    lanes = slice(g * POOL_GROUP_DIM, (g + 1) * POOL_GROUP_DIM)
                    folded = jnp.dot(wp_ref[g] * ps_ref[:, lanes], wout_ref[lr:lr + POOL_GROUP_DIM, :],
                                     precision=lax.Precision.HIGHEST,
                                     preferred_element_type=jnp.float32)
                    wout_s[r0 + lr:r0 + lr + POOL_GROUP_DIM, :] = folded.astype(jnp.bfloat16)


def _mixer_consts(wcat_ref):
    row_head = lax.broadcasted_iota(jnp.int32, (V7X_MXU_DIM, V7X_MXU_DIM), 0) // A_HEAD_DIM
    col_head = lax.broadcasted_iota(jnp.int32, (V7X_MXU_DIM, V7X_MXU_DIM), 1) // A_HEAD_DIM
    seg = (row_head == col_head).astype(jnp.bfloat16)
    wrow = lax.broadcasted_iota(jnp.int32, wcat_ref.shape, 0)
    wcol = lax.broadcasted_iota(jnp.int32, wcat_ref.shape, 1) % CHUNK
    wcat = jnp.where(wcol <= wrow, wcat_ref[...], 0.0).astype(jnp.bfloat16)
    lane_head = lax.broadcasted_iota(jnp.int32, (CHUNK, V7X_MXU_DIM), 1) // A_HEAD_DIM
    head_masks = [(lane_head == hh).astype(jnp.bfloat16) for hh in range(V7X_MXU_DIM // A_HEAD_DIM)]
    return seg, wcat, head_masks


def _mixer_kernel(x_ref, gm_ref, win_ref, vg_ref, wcat_ref, bias_ref, wp_ref, ps_ref, wout_ref,
                  o_ref, win_s, wout_s, zprev_ref, *, tiles_per_seq):
    i = pl.program_id(0)

    @pl.when(i < M_PREP_STEPS)
    def _():
        _store_bf16_chunk(i, win_ref, win_s)

    _prep_wout(i, M_PREP_STEPS, wout_ref, wp_ref, ps_ref, wout_s)

    @pl.when(i >= M_PREP_STEPS)
    def _():
        tm = x_ref.shape[0]
        j = (i - M_PREP_STEPS) % tiles_per_seq
        consts = _mixer_consts(wcat_ref)

        @pl.when(j == 0)
        def _():
            zprev_ref[...] = jnp.zeros_like(zprev_ref)

        halo = zprev_ref[...]
        for r in range(0, tm, MIX_ROWS):
            x2, halo = _mixer_rows(x_ref[r:r + MIX_ROWS, :], halo, j * tm + (r + 1), consts,
                                   gm_ref, win_s, vg_ref, bias_ref, wout_s)
            o_ref[r:r + MIX_ROWS, :] = x2
        zprev_ref[...] = halo


def _mixer_call(x3d, gm, w_in, v_gain, wcat, bias, w_pool, p_scale, w_out):
    b, s, d = x3d.shape
    d_in, d_mix = w_in.shape[1], w_out.shape[0]
    a_width = A_HEADS * A_HEAD_DIM
    b_width = len(POOL_WINDOWS) * POOL_GROUP_DIM
    assert s % MIX_TOKENS == 0 and MIX_TOKENS % MIX_ROWS == 0 and MIX_ROWS % CHUNK == 0
    assert d_in == 2 * a_width + b_width and d_mix == a_width + b_width
    assert d % (M_PREP_STEPS * BF16_SUBLANES) == 0 and d_mix % (M_PREP_STEPS * BF16_SUBLANES) == 0
    assert (d_mix // M_PREP_STEPS) % POOL_GROUP_DIM == 0 and a_width % (d_mix // M_PREP_STEPS) == 0
    tiles_per_seq = s // MIX_TOKENS
    full = lambda i: (0, 0)
    chunk = lambda i: (jnp.minimum(i, M_PREP_STEPS - 1), 0)

    def tile(i):
        t = jnp.maximum(i - M_PREP_STEPS, 0)
        return (t // tiles_per_seq, t % tiles_per_seq, 0)

    return pl.pallas_call(
        functools.partial(_mixer_kernel, tiles_per_seq=tiles_per_seq),
        grid=(M_PREP_STEPS + b * tiles_per_seq,),
        in_specs=[
            pl.BlockSpec((None, MIX_TOKENS, d), tile),
            pl.BlockSpec((1, d), full),
            pl.BlockSpec((d // M_PREP_STEPS, d_in), chunk),
            pl.BlockSpec((1, a_width), full),
            pl.BlockSpec(wcat.shape, full),
            pl.BlockSpec(bias.shape, full),
            pl.BlockSpec(w_pool.shape, lambda i: (0, 0, 0)),
            pl.BlockSpec((1, b_width), full),
            pl.BlockSpec((d_mix // M_PREP_STEPS, d), chunk),
        ],
        out_specs=pl.BlockSpec((None, MIX_TOKENS, d), tile),
        out_shape=jax.ShapeDtypeStruct((b, s, d), jnp.float32),
        scratch_shapes=[pltpu.VMEM((d, d_in), jnp.bfloat16),
                        pltpu.VMEM((d_mix, d), jnp.bfloat16),
                        pltpu.VMEM((HALO, b_width), jnp.float32)],
        compiler_params=pltpu.CompilerParams(
            dimension_semantics=("arbitrary",),
            vmem_limit_bytes=V7X_VMEM_LIMIT_BYTES),
        name="mixer",
    )(x3d, gm, w_in, v_gain, wcat, bias, w_pool, p_scale, w_out)


def _ffn_kernel(x_ref, g_ref, wga_ref, wgb_ref, wua_ref, wub_ref, wda_ref, wdb_ref, fg_ref, o_ref,
                wg_s, wu_s, wd_s, h_s, acc_s, *, final_norm):
    i = pl.program_id(0)
    n_chunks = wg_s.shape[0]

    @pl.when(i == 0)
    def _():
        h_s[...] = _rmsnorm(x_ref[...], g_ref[...]).astype(jnp.bfloat16)
        acc_s[...] = jnp.zeros_like(acc_s)

    @pl.when(i < n_chunks)
    def _():
        wg = jnp.concatenate([wga_ref[...], wgb_ref[...]], axis=0).astype(jnp.bfloat16)
        wu = jnp.concatenate([wua_ref[...], wub_ref[...]], axis=0).astype(jnp.bfloat16)
        wd = jnp.concatenate([wda_ref[...], wdb_ref[...]], axis=0).astype(jnp.bfloat16)
        wg_s[i] = wg
        wu_s[i] = wu
        wd_s[pl.ds(pl.multiple_of(i * FFN_COLS, FFN_COLS), FFN_COLS), :] = wd
        h = h_s[...]
        a = (_silu(_dot(h, wg)) * _dot(h, wu)).astype(jnp.bfloat16)
        acc_s[...] += _dot(a, wd)

    @pl.when(i == n_chunks - 1)
    def _():
        y = x_ref[...] + 0.5 * acc_s[...]
        if final_norm:
            y = _rmsnorm(y, fg_ref[...])
        o_ref[...] = y

    @pl.when(i >= n_chunks)
    def _():
        r = 0
        for rows in FFN_ROW_BLOCKS:
            y = _ffn_rows(x_ref[r:r + rows, :], g_ref, wg_s, wu_s, wd_s)
            if final_norm:
                y = _rmsnorm(y, fg_ref[...])
            o_ref[r:r + rows, :] = y
            r += rows


def _ffn_call(x2d, g, wg, wu, wd, fg, *, final_norm):
    n, d = x2d.shape
    d_ff = wg.shape[1]
    assert n % FFN_TOKENS == 0 and sum(FFN_ROW_BLOCKS) == FFN_TOKENS and d_ff % FFN_COLS == 0
    n_chunks = d_ff // FFN_COLS
    full = lambda i: (0, 0)
    tile = lambda i: (jnp.maximum(i - (n_chunks - 1), 0), 0)
    chunk = lambda i: jnp.minimum(i, n_chunks - 1)
    col_top, col_bot = (lambda i: (0, chunk(i))), (lambda i: (1, chunk(i)))
    row_top, row_bot = (lambda i: (2 * chunk(i), 0)), (lambda i: (2 * chunk(i) + 1, 0))
    return pl.pallas_call(
        functools.partial(_ffn_kernel, final_norm=final_norm),
        grid=(n_chunks - 1 + n // FFN_TOKENS,),
        in_specs=[
            pl.BlockSpec((FFN_TOKENS, d), tile),
            pl.BlockSpec((1, d), full),
            pl.BlockSpec((d // 2, FFN_COLS), col_top),
            pl.BlockSpec((d // 2, FFN_COLS), col_bot),
            pl.BlockSpec((d // 2, FFN_COLS), col_top),
            pl.BlockSpec((d // 2, FFN_COLS), col_bot),
            pl.BlockSpec((FFN_COLS // 2, d), row_top),
            pl.BlockSpec((FFN_COLS // 2, d), row_bot),
            pl.BlockSpec((1, d), full),
        ],
        out_specs=pl.BlockSpec((FFN_TOKENS, d), tile),
        out_shape=jax.ShapeDtypeStruct((n, d), jnp.float32),
        scratch_shapes=[pltpu.VMEM((n_chunks, d, FFN_COLS), jnp.bfloat16),
                        pltpu.VMEM((n_chunks, d, FFN_COLS), jnp.bfloat16),
                        pltpu.VMEM((d_ff, d), jnp.bfloat16),
                        pltpu.VMEM((FFN_TOKENS, d), jnp.bfloat16),
                        pltpu.VMEM((FFN_TOKENS, d), jnp.float32)],
        compiler_params=pltpu.CompilerParams(
            dimension_semantics=("arbitrary",),
            vmem_limit_bytes=V7X_VMEM_LIMIT_BYTES),
        name="ffn_final" if final_norm else "ffn",
    )(x2d, g, wg, wg, wu, wu, wd, wd, fg)


def kernel(x, ffn1_norm, ffn1_w_gate, ffn1_w_up, ffn1_w_down, mix_norm, w_in, gmlp_v_norm,
           gmlp_w_s, gmlp_b_s, pool_w, pool_scale, w_out, ffn2_norm, ffn2_w_gate, ffn2_w_up,
           ffn2_w_down, final_norm):
    bsz, seq, d = x.shape
    depth = ffn1_norm.shape[0]
    fg = final_norm.reshape(1, d)
    for l in range(depth):
        last = l == depth - 1
        x2d = _ffn_call(x.reshape(bsz * seq, d), ffn1_norm[l].reshape(1, d), ffn1_w_gate[l],
                        ffn1_w_up[l], ffn1_w_down[l], fg, final_norm=False)
        wcat = jnp.transpose(gmlp_w_s[l], (1, 0, 2)).reshape(CHUNK, A_HEADS * CHUNK)
        bias = jnp.repeat(gmlp_b_s[l].T, A_HEAD_DIM, axis=1)
        x3d = _mixer_call(x2d.reshape(bsz, seq, d), mix_norm[l].reshape(1, d), w_in[l],
                          gmlp_v_norm[l].reshape(1, -1), wcat, bias, pool_w[l],
                          pool_scale[l].reshape(1, -1), w_out[l])
        x = _ffn_call(x3d.reshape(bsz * seq, d), ffn2_norm[l].reshape(1, d), ffn2_w_gate[l],
                      ffn2_w_up[l], ffn2_w_down[l], fg, final_norm=last).reshape(bsz, seq, d)
    return x
```

```python
import functools

import jax
import jax.numpy as jnp
from jax import lax
from jax.experimental import pallas as pl
from jax.experimental.pallas import tpu as pltpu

EPS = 1e-6
CHUNK = 128
A_HEADS = 8
A_HEAD_DIM = 64
POOL_WINDOWS = (2, 4, 8, 16)
POOL_GROUP_DIM = 128
HALO = 16
assert all(w & (w - 1) == 0 and w <= HALO for w in POOL_WINDOWS)

V7X_MXU_DIM = 256
V7X_VMEM_LIMIT_BYTES = 56 * 1024 * 1024
BF16_SUBLANES = 16

FFN_TOKENS = 1024
MIX_TOKENS = 1024
FFN_ROW_BLOCKS = (256, 256, 256, 256)
MIX_ROWS = 512
FFN_COLS = 256
M_PREP_STEPS = 2


def _rmsnorm(x, g):
    r = lax.rsqrt(jnp.mean(x * x, axis=-1, keepdims=True) + EPS)
    return (x * r) * g


def _silu(x):
    return x * (0.5 * jnp.tanh(0.5 * x) + 0.5)


def _gelu_tanh(x):
    c = 0.7978845608028654
    return 0.5 * x * (1.0 + jnp.tanh(c * (x + 0.044715 * (x * x * x))))


def _dot(a, b):
    return jnp.dot(a, b, preferred_element_type=jnp.float32)


def _to_bf16(a):
    return a.astype(jnp.bfloat16)


def _store_bf16_chunk(i, src_ref, dst_ref):
    rows = src_ref.shape[0]
    r0 = pl.multiple_of(i * rows, rows)
    dst_ref[pl.ds(r0, rows), :] = src_ref[...].astype(jnp.bfloat16)


def _ffn_rows(x, g_ref, wg_s, wu_s, wd_s):
    h = _rmsnorm(x, g_ref[...]).astype(jnp.bfloat16)
    acts = []
    for c in range(wg_s.shape[0]):
        gate = _dot(h, wg_s[c])
        up = _dot(h, wu_s[c])
        acts.append((_silu(gate) * up).astype(jnp.bfloat16))
    a = jnp.concatenate(acts, axis=1)
    return x + _dot(a, wd_s[...])


def _mixer_rows(x, halo, first_pos, consts, g_ref, win_s, vg_ref, bias_ref, wout_s):
    seg, wcat, head_masks = consts
    rows = x.shape[0]
    a_width = A_HEADS * A_HEAD_DIM
    heads_per_group = len(head_masks)

    h = _rmsnorm(x, g_ref[...]).astype(jnp.bfloat16)
    p = _dot(h, win_s[...])
    u = _gelu_tanh(p[:, :a_width])
    v = _gelu_tanh(p[:, a_width:2 * a_width])
    z = p[:, 2 * a_width:]

    vv = (v * v).astype(jnp.bfloat16)
    ms = jnp.concatenate(
        [_dot(vv[:, gs:gs + V7X_MXU_DIM], seg) for gs in range(0, a_width, V7X_MXU_DIM)],
        axis=1) * (1.0 / A_HEAD_DIM)
    vn = ((v * lax.rsqrt(ms + EPS)) * vg_ref[...]).astype(jnp.bfloat16)

    ya_chunks = []
    for c in range(0, rows, CHUNK):
        groups = []
        for gi in range(a_width // V7X_MXU_DIM):
            vg = vn[c:c + CHUNK, gi * V7X_MXU_DIM:(gi + 1) * V7X_MXU_DIM]
            stack = jnp.concatenate([vg * m for m in head_masks], axis=0)
            k0 = gi * heads_per_group * CHUNK
            groups.append(_dot(wcat[:, k0:k0 + heads_per_group * CHUNK], stack))
        mixed = jnp.concatenate(groups, axis=1) + bias_ref[...]
        ya_chunks.append(u[c:c + CHUNK, :] * mixed)
    y_a = jnp.concatenate(ya_chunks, axis=0)

    zext = jnp.concatenate([halo, z], axis=0)
    head_pos = first_pos + lax.broadcasted_iota(jnp.int32, (HALO, POOL_GROUP_DIM), 0)
    d_groups = []
    for gi, win in enumerate(POOL_WINDOWS):
        lanes = slice(gi * POOL_GROUP_DIM, (gi + 1) * POOL_GROUP_DIM)
        s = zext[:, lanes]
        step = 1
        while step < win:
            s = s + pltpu.roll(s, step, 0)
            step *= 2
        head_count = jnp.minimum(head_pos, win).astype(jnp.float32)
        mean = jnp.concatenate([s[HALO:2 * HALO, :] / head_count,
                                s[2 * HALO:, :] * (1.0 / win)], axis=0)
        d_groups.append((mean - z[:, lanes]).astype(jnp.bfloat16))

    y = jnp.concatenate([y_a.astype(jnp.bfloat16)] + d_groups, axis=1)
    return x + _dot(y, wout_s[...]), z[rows - HALO:, :]


def _prep_wout(i, n_steps, wout_ref, wp_ref, ps_ref, wout_s):
    a_width = A_HEADS * A_HEAD_DIM
    chunk_rows = wout_ref.shape[0]
    for step in range(n_steps):
        @pl.when(i == step)
        def _(step=step):
            r0 = step * chunk_rows
            if r0 < a_width:
                wout_s[r0:r0 + chunk_rows, :] = wout_ref[...].astype(jnp.bfloat16)
            else:
                for lr in range(0, chunk_rows, POOL_GROUP_DIM):
                    g = (r0 + lr - a_width) // POOL_GROUP_DIM
                    lanes = slice(g * POOL_GROUP_DIM, (g + 1) * POOL_GROUP_DIM)
                    folded = jnp.dot(wp_ref[g] * ps_ref[:, lanes], wout_ref[lr:lr + POOL_GROUP_DIM, :],
                                     precision=lax.Precision.HIGHEST,
                                     preferred_element_type=jnp.float32)
                    wout_s[r0 + lr:r0 + lr + POOL_GROUP_DIM, :] = folded.astype(jnp.bfloat16)


def _mixer_consts(wcat_ref):
    row_head = lax.broadcasted_iota(jnp.int32, (V7X_MXU_DIM, V7X_MXU_DIM), 0) // A_HEAD_DIM
    col_head = lax.broadcasted_iota(jnp.int32, (V7X_MXU_DIM, V7X_MXU_DIM), 1) // A_HEAD_DIM
    seg = (row_head == col_head).astype(jnp.bfloat16)
    wrow = lax.broadcasted_iota(jnp.int32, wcat_ref.shape, 0)
    wcol = lax.broadcasted_iota(jnp.int32, wcat_ref.shape, 1) % CHUNK
    wcat = jnp.where(wcol <= wrow, wcat_ref[...], 0.0).astype(jnp.bfloat16)
    lane_head = lax.broadcasted_iota(jnp.int32, (CHUNK, V7X_MXU_DIM), 1) // A_HEAD_DIM
    head_masks = [(lane_head == hh).astype(jnp.bfloat16) for hh in range(V7X_MXU_DIM // A_HEAD_DIM)]
    return seg, wcat, head_masks


def _mixer_kernel(x_ref, gm_ref, win_ref, vg_ref, wcat_ref, bias_ref, wp_ref, ps_ref, wout_ref,
                  o_ref, win_s, wout_s, zprev_ref, *, tiles_per_seq):
    i = pl.program_id(0)

    @pl.when(i < M_PREP_STEPS)
    def _():
        _store_bf16_chunk(i, win_ref, win_s)

    _prep_wout(i, M_PREP_STEPS, wout_ref, wp_ref, ps_ref, wout_s)

    @pl.when(i >= M_PREP_STEPS)
    def _():
        tm = x_ref.shape[0]
        j = (i - M_PREP_STEPS) % tiles_per_seq
        consts = _mixer_consts(wcat_ref)

        @pl.when(j == 0)
        def _():
            zprev_ref[...] = jnp.zeros_like(zprev_ref)

        halo = zprev_ref[...]
        for r in range(0, tm, MIX_ROWS):
            x2, halo = _mixer_rows(x_ref[r:r + MIX_ROWS, :], halo, j * tm + (r + 1), consts,
                                   gm_ref, win_s, vg_ref, bias_ref, wout_s)
            o_ref[r:r + MIX_ROWS, :] = x2
        zprev_ref[...] = halo


def _mixer_call(x3d, gm, w_in, v_gain, wcat, bias, w_pool, p_scale, w_out):
    b, s, d = x3d.shape
    d_in, d_mix = w_in.shape[1], w_out.shape[0]
    a_width = A_HEADS * A_HEAD_DIM
    b_width = len(POOL_WINDOWS) * POOL_GROUP_DIM
    assert s % MIX_TOKENS == 0 and MIX_TOKENS % MIX_ROWS == 0 and MIX_ROWS % CHUNK == 0
    assert d_in == 2 * a_width + b_width and d_mix == a_width + b_width
    assert d % (M_PREP_STEPS * BF16_SUBLANES) == 0 and d_mix % (M_PREP_STEPS * BF16_SUBLANES) == 0
    assert (d_mix // M_PREP_STEPS) % POOL_GROUP_DIM == 0 and a_width % (d_mix // M_PREP_STEPS) == 0
    tiles_per_seq = s // MIX_TOKENS
    full = lambda i: (0, 0)
    chunk = lambda i: (jnp.minimum(i, M_PREP_STEPS - 1), 0)

    def tile(i):
        t = jnp.maximum(i - M_PREP_STEPS, 0)
        return (t // tiles_per_seq, t % tiles_per_seq, 0)

    return pl.pallas_call(
        functools.partial(_mixer_kernel, tiles_per_seq=tiles_per_seq),
        grid=(M_PREP_STEPS + b * tiles_per_seq,),
        in_specs=[
            pl.BlockSpec((None, MIX_TOKENS, d), tile),
            pl.BlockSpec((1, d), full),
            pl.BlockSpec((d // M_PREP_STEPS, d_in), chunk),
            pl.BlockSpec((1, a_width), full),
            pl.BlockSpec(wcat.shape, full),
            pl.BlockSpec(bias.shape, full),
            pl.BlockSpec(w_pool.shape, lambda i: (0, 0, 0)),
            pl.BlockSpec((1, b_width), full),
            pl.BlockSpec((d_mix // M_PREP_STEPS, d), chunk),
        ],
        out_specs=pl.BlockSpec((None, MIX_TOKENS, d), tile),
        out_shape=jax.ShapeDtypeStruct((b, s, d), jnp.float32),
        scratch_shapes=[pltpu.VMEM((d, d_in), jnp.bfloat16),
                        pltpu.VMEM((d_mix, d), jnp.bfloat16),
                        pltpu.VMEM((HALO, b_width), jnp.float32)],
        compiler_params=pltpu.CompilerParams(
            dimension_semantics=("arbitrary",),
            vmem_limit_bytes=V7X_VMEM_LIMIT_BYTES),
        name="mixer",
    )(x3d, gm, w_in, v_gain, wcat, bias, w_pool, p_scale, w_out)


def _ffn_kernel(*refs, final_norm, weights_ready, convert_next):
    n_w = 3 if weights_ready else 6
    n_next = 3 if convert_next else 0
    x_ref, g_ref = refs[:2]
    w_refs = refs[2:2 + n_w]
    fg_ref = refs[2 + n_w]
    next_in = refs[3 + n_w:3 + n_w + n_next]
    o_ref = refs[3 + n_w + n_next]
    next_out = refs[4 + n_w + n_next:4 + n_w + 2 * n_next]
    wg_s, wu_s, wd_s, h_s, acc_s = refs[4 + n_w + 2 * n_next:]
    i = pl.program_id(0)
    n_chunks = wg_s.shape[0]

    @pl.when(i == 0)
    def _():
        h_s[...] = _rmsnorm(x_ref[...], g_ref[...]).astype(jnp.bfloat16)
        acc_s[...] = jnp.zeros_like(acc_s)

    @pl.when(i < n_chunks)
    def _():
        if weights_ready:
            wg, wu, wd = (r[...] for r in w_refs)
        else:
            wg, wu, wd = (jnp.concatenate([w_refs[2 * k][...], w_refs[2 * k + 1][...]], axis=0)
                          for k in range(3))
            wg, wu, wd = _to_bf16(wg), _to_bf16(wu), _to_bf16(0.5 * wd)
        wg_s[i] = wg
        wu_s[i] = wu
        wd_s[pl.ds(pl.multiple_of(i * FFN_COLS, FFN_COLS), FFN_COLS), :] = wd
        h = h_s[...]
        a = _to_bf16(_silu(_dot(h, wg)) * _dot(h, wu))
        acc_s[...] += _dot(a, wd)

    @pl.when(i == n_chunks - 1)
    def _():
        y = x_ref[...] + acc_s[...]
        if final_norm:
            y = _rmsnorm(y, fg_ref[...])
        o_ref[...] = y

    @pl.when(i >= n_chunks)
    def _():
        r = 0
        for rows in FFN_ROW_BLOCKS:
            y = _ffn_rows(x_ref[r:r + rows, :], g_ref, wg_s, wu_s, wd_s)
            if final_norm:
                y = _rmsnorm(y, fg_ref[...])
            o_ref[r:r + rows, :] = y
            r += rows
        if convert_next:
            next_out[0][...] = _to_bf16(next_in[0][...])
            next_out[1][...] = _to_bf16(next_in[1][...])
            next_out[2][...] = _to_bf16(0.5 * next_in[2][...])


def _ffn_call(x2d, g, weights, fg, *, final_norm, weights_ready=False, next_weights=None):
    n, d = x2d.shape
    d_ff = weights[2].shape[0]
    assert n % FFN_TOKENS == 0 and sum(FFN_ROW_BLOCKS) == FFN_TOKENS and d_ff % FFN_COLS == 0
    n_chunks, n_tiles = d_ff // FFN_COLS, n // FFN_TOKENS
    assert n_tiles - 1 >= n_chunks
    full = lambda i: (0, 0)
    tile = lambda i: (jnp.maximum(i - (n_chunks - 1), 0), 0)
    chunk = lambda i: jnp.minimum(i, n_chunks - 1)
    if weights_ready:
        w_specs = [pl.BlockSpec((None, d, FFN_COLS), lambda i: (chunk(i), 0, 0)),
                   pl.BlockSpec((None, d, FFN_COLS), lambda i: (chunk(i), 0, 0)),
                   pl.BlockSpec((FFN_COLS, d), lambda i: (chunk(i), 0))]
        w_args = list(weights)
    else:
        col_top, col_bot = (lambda i: (0, chunk(i))), (lambda i: (1, chunk(i)))
        row_top, row_bot = (lambda i: (2 * chunk(i), 0)), (lambda i: (2 * chunk(i) + 1, 0))
        w_specs = [pl.BlockSpec((d // 2, FFN_COLS), col_top), pl.BlockSpec((d // 2, FFN_COLS), col_bot),
                   pl.BlockSpec((d // 2, FFN_COLS), col_top), pl.BlockSpec((d // 2, FFN_COLS), col_bot),
                   pl.BlockSpec((FFN_COLS // 2, d), row_top), pl.BlockSpec((FFN_COLS // 2, d), row_bot)]
        w_args = [weights[0], weights[0], weights[1], weights[1], weights[2], weights[2]]
    out_specs = [pl.BlockSpec((FFN_TOKENS, d), tile)]
    out_shape = [jax.ShapeDtypeStruct((n, d), jnp.float32)]
    next_specs, next_args = [], []
    if next_weights is not None:
        nchunk = lambda i: jnp.clip(i - n_chunks, 0, n_chunks - 1)
        next_specs = [pl.BlockSpec((d, FFN_COLS), lambda i: (0, nchunk(i))),
                      pl.BlockSpec((d, FFN_COLS), lambda i: (0, nchunk(i))),
                      pl.BlockSpec((FFN_COLS, d), lambda i: (nchunk(i), 0))]
        next_args = list(next_weights)
        out_specs += [pl.BlockSpec((None, d, FFN_COLS), lambda i: (nchunk(i), 0, 0)),
                      pl.BlockSpec((None, d, FFN_COLS), lambda i: (nchunk(i), 0, 0)),
                      pl.BlockSpec((FFN_COLS, d), lambda i: (nchunk(i), 0))]
        out_shape += [jax.ShapeDtypeStruct((n_chunks, d, FFN_COLS), jnp.bfloat16),
                      jax.ShapeDtypeStruct((n_chunks, d, FFN_COLS), jnp.bfloat16),
                      jax.ShapeDtypeStruct((d_ff, d), jnp.bfloat16)]
    outs = pl.pallas_call(
        functools.partial(_ffn_kernel, final_norm=final_norm, weights_ready=weights_ready,
                          convert_next=next_weights is not None),
        grid=(n_chunks - 1 + n_tiles,),
        in_specs=[pl.BlockSpec((FFN_TOKENS, d), tile), pl.BlockSpec((1, d), full)] + w_specs
                 + [pl.BlockSpec((1, d), full)] + next_specs,
        out_specs=out_specs,
        out_shape=out_shape,
        scratch_shapes=[pltpu.VMEM((n_chunks, d, FFN_COLS), jnp.bfloat16),
                        pltpu.VMEM((n_chunks, d, FFN_COLS), jnp.bfloat16),
                        pltpu.VMEM((d_ff, d), jnp.bfloat16),
                        pltpu.VMEM((FFN_TOKENS, d), jnp.bfloat16),
                        pltpu.VMEM((FFN_TOKENS, d), jnp.float32)],
        compiler_params=pltpu.CompilerParams(
            dimension_semantics=("arbitrary",),
            vmem_limit_bytes=V7X_VMEM_LIMIT_BYTES),
        name="ffn_final" if final_norm else "ffn",
    )(x2d, g, *w_args, fg, *next_args)
    return outs[0], tuple(outs[1:])


def kernel(x, ffn1_norm, ffn1_w_gate, ffn1_w_up, ffn1_w_down, mix_norm, w_in, gmlp_v_norm,
           gmlp_w_s, gmlp_b_s, pool_w, pool_scale, w_out, ffn2_norm, ffn2_w_gate, ffn2_w_up,
           ffn2_w_down, final_norm):
    bsz, seq, d = x.shape
    depth = ffn1_norm.shape[0]
    fg = final_norm.reshape(1, d)
    for l in range(depth):
        last = l == depth - 1
        x2d, ffn2_w = _ffn_call(x.reshape(bsz * seq, d), ffn1_norm[l].reshape(1, d),
                                (ffn1_w_gate[l], ffn1_w_up[l], ffn1_w_down[l]), fg, final_norm=False,
                                next_weights=(ffn2_w_gate[l], ffn2_w_up[l], ffn2_w_down[l]))
        wcat = jnp.transpose(gmlp_w_s[l], (1, 0, 2)).reshape(CHUNK, A_HEADS * CHUNK)
        bias = jnp.repeat(gmlp_b_s[l].T, A_HEAD_DIM, axis=1)
        x3d = _mixer_call(x2d.reshape(bsz, seq, d), mix_norm[l].reshape(1, d), w_in[l],
                          gmlp_v_norm[l].reshape(1, -1), wcat, bias, pool_w[l],
                          pool_scale[l].reshape(1, -1), w_out[l])
        x, _ = _ffn_call(x3d.reshape(bsz * seq, d), ffn2_norm[l].reshape(1, d), ffn2_w, fg,
                         final_norm=last, weights_ready=True)
        x = x.reshape(bsz, seq, d)
    return x
```

```python
import functools

import jax
import jax.numpy as jnp
from jax import lax
from jax.experimental import pallas as pl
from jax.experimental.pallas import tpu as pltpu

EPS = 1e-6
CHUNK = 128
A_HEADS = 8
A_HEAD_DIM = 64
POOL_WINDOWS = (2, 4, 8, 16)
POOL_GROUP_DIM = 128
HALO = 16
assert all(w & (w - 1) == 0 and w <= HALO for w in POOL_WINDOWS)

V7X_MXU_DIM = 256
V7X_VMEM_LIMIT_BYTES = 56 * 1024 * 1024
BF16_SUBLANES = 16

FFN_TOKENS = 1024
MIX_TOKENS = 1024
FFN_ROW_BLOCKS = (256, 256, 256, 128, 128)
MIX_ROWS = 512
FFN_COLS = 256
M_PREP_STEPS = 2


def _rmsnorm(x, g):
    r = lax.rsqrt(jnp.mean(x * x, axis=-1, keepdims=True) + EPS)
    return (x * r) * g


def _silu(x):
    return x * (0.5 * jnp.tanh(0.5 * x) + 0.5)


def _gelu_tanh(x):
    c = 0.7978845608028654
    return 0.5 * x * (1.0 + jnp.tanh(c * (x + 0.044715 * (x * x * x))))


def _dot(a, b):
    return jnp.dot(a, b, preferred_element_type=jnp.float32)


def _store_bf16_chunk(i, src_ref, dst_ref):
    rows = src_ref.shape[0]
    r0 = pl.multiple_of(i * rows, rows)
    dst_ref[pl.ds(r0, rows), :] = src_ref[...].astype(jnp.bfloat16)


def _ffn_rows(x, g_ref, wg_s, wu_s, wd_s):
    h = _rmsnorm(x, g_ref[...]).astype(jnp.bfloat16)
    acts = []
    for c in range(wg_s.shape[0]):
        gate = _dot(h, wg_s[c])
        up = _dot(h, wu_s[c])
        acts.append((_silu(gate) * up).astype(jnp.bfloat16))
    a = jnp.concatenate(acts, axis=1)
    return x + _dot(a, wd_s[...])


def _mixer_rows(x, halo, first_pos, consts, g_ref, win_s, vg_ref, bias_ref, wout_s):
    seg, wcat, head_masks = consts
    rows = x.shape[0]
    a_width = A_HEADS * A_HEAD_DIM
    heads_per_group = len(head_masks)

    h = _rmsnorm(x, g_ref[...]).astype(jnp.bfloat16)
    p = _dot(h, win_s[...])
    u = _gelu_tanh(p[:, :a_width])
    v = _gelu_tanh(p[:, a_width:2 * a_width])
    z = p[:, 2 * a_width:]

    vv = (v * v).astype(jnp.bfloat16)
    ms = jnp.concatenate(
        [_dot(vv[:, gs:gs + V7X_MXU_DIM], seg) for gs in range(0, a_width, V7X_MXU_DIM)],
        axis=1) * (1.0 / A_HEAD_DIM)
    vn = ((v * lax.rsqrt(ms + EPS)) * vg_ref[...]).astype(jnp.bfloat16)

    ya_chunks = []
    for c in range(0, rows, CHUNK):
        groups = []
        for gi in range(a_width // V7X_MXU_DIM):
            vg = vn[c:c + CHUNK, gi * V7X_MXU_DIM:(gi + 1) * V7X_MXU_DIM]
            stack = jnp.concatenate([vg * m for m in head_masks], axis=0)
            k0 = gi * heads_per_group * CHUNK
            groups.append(_dot(wcat[:, k0:k0 + heads_per_group * CHUNK], stack))
        mixed = jnp.concatenate(groups, axis=1) + bias_ref[...]
        ya_chunks.append(u[c:c + CHUNK, :] * mixed)
    y_a = jnp.concatenate(ya_chunks, axis=0)

    zext = jnp.concatenate([halo, z], axis=0)
    head_pos = first_pos + lax.broadcasted_iota(jnp.int32, (HALO, POOL_GROUP_DIM), 0)
    d_groups = []
    for gi, win in enumerate(POOL_WINDOWS):
        lanes = slice(gi * POOL_GROUP_DIM, (gi + 1) * POOL_GROUP_DIM)
        s = zext[:, lanes]
        step = 1
        while step < win:
            s = s + pltpu.roll(s, step, 0)
            step *= 2
        head_count = jnp.minimum(head_pos, win).astype(jnp.float32)
        mean = jnp.concatenate([s[HALO:2 * HALO, :] / head_count,
                                s[2 * HALO:, :] * (1.0 / win)], axis=0)
        d_groups.append((mean - z[:, lanes]).astype(jnp.bfloat16))

    y = jnp.concatenate([y_a.astype(jnp.bfloat16)] + d_groups, axis=1)
    return x + _dot(y, wout_s[...]), z[rows - HALO:, :]


def _prep_wout(i, n_steps, wout_ref, wp_ref, ps_ref, wout_s):
    a_width = A_HEADS * A_HEAD_DIM
    chunk_rows = wout_ref.shape[0]
    for step in range(n_steps):
        @pl.when(i == step)
        def _(step=step):
            r0 = step * chunk_rows
            if r0 < a_width:
                wout_s[r0:r0 + chunk_rows, :] = wout_ref[...].astype(jnp.bfloat16)
            else:
                for lr in range(0, chunk_rows, POOL_GROUP_DIM):
                    g = (r0 + lr - a_width) // POOL_GROUP_DIM
                    lanes = slice(g * POOL_GROUP_DIM, (g + 1) * POOL_GROUP_DIM)
                    folded = jnp.dot(wp_ref[g] * ps_ref[:, lanes], wout_ref[lr:lr + POOL_GROUP_DIM, :],
                                     precision=lax.Precision.HIGHEST,
                                     preferred_element_type=jnp.float32)
                    wout_s[r0 + lr:r0 + lr + POOL_GROUP_DIM, :] = folded.astype(jnp.bfloat16)


def _mixer_consts(wcat_ref):
    row_head = lax.broadcasted_iota(jnp.int32, (V7X_MXU_DIM, V7X_MXU_DIM), 0) // A_HEAD_DIM
    col_head = lax.broadcasted_iota(jnp.int32, (V7X_MXU_DIM, V7X_MXU_DIM), 1) // A_HEAD_DIM
    seg = (row_head == col_head).astype(jnp.bfloat16)
    wrow = lax.broadcasted_iota(jnp.int32, wcat_ref.shape, 0)
    wcol = lax.broadcasted_iota(jnp.int32, wcat_ref.shape, 1) % CHUNK
    wcat = jnp.where(wcol <= wrow, wcat_ref[...], 0.0).astype(jnp.bfloat16)
    lane_head = lax.broadcasted_iota(jnp.int32, (CHUNK, V7X_MXU_DIM), 1) // A_HEAD_DIM
    head_masks = [(lane_head == hh).astype(jnp.bfloat16) for hh in range(V7X_MXU_DIM // A_HEAD_DIM)]
    return seg, wcat, head_masks


def _mixer_kernel(x_ref, gm_ref, win_ref, vg_ref, wcat_ref, bias_ref, wp_ref, ps_ref, wout_ref,
                  o_ref, win_s, wout_s, zprev_ref, *, tiles_per_seq):
    i = pl.program_id(0)

    @pl.when(i < M_PREP_STEPS)
    def _():
        _store_bf16_chunk(i, win_ref, win_s)

    _prep_wout(i, M_PREP_STEPS, wout_ref, wp_ref, ps_ref, wout_s)

    @pl.when(i >= M_PREP_STEPS)
    def _():
        tm = x_ref.shape[0]
        j = (i - M_PREP_STEPS) % tiles_per_seq
        consts = _mixer_consts(wcat_ref)

        @pl.when(j == 0)
        def _():
            zprev_ref[...] = jnp.zeros_like(zprev_ref)

        halo = zprev_ref[...]
        for r in range(0, tm, MIX_ROWS):
            x2, halo = _mixer_rows(x_ref[r:r + MIX_ROWS, :], halo, j * tm + (r + 1), consts,
                                   gm_ref, win_s, vg_ref, bias_ref, wout_s)
            o_ref[r:r + MIX_ROWS, :] = x2
        zprev_ref[...] = halo


def _mixer_call(x3d, gm, w_in, v_gain, wcat, bias, w_pool, p_scale, w_out):
    b, s, d = x3d.shape
    d_in, d_mix = w_in.shape[1], w_out.shape[0]
    a_width = A_HEADS * A_HEAD_DIM
    b_width = len(POOL_WINDOWS) * POOL_GROUP_DIM
    assert s % MIX_TOKENS == 0 and MIX_TOKENS % MIX_ROWS == 0 and MIX_ROWS % CHUNK == 0
    assert d_in == 2 * a_width + b_width and d_mix == a_width + b_width
    assert d % (M_PREP_STEPS * BF16_SUBLANES) == 0 and d_mix % (M_PREP_STEPS * BF16_SUBLANES) == 0
    assert (d_mix // M_PREP_STEPS) % POOL_GROUP_DIM == 0 and a_width % (d_mix // M_PREP_STEPS) == 0
    tiles_per_seq = s // MIX_TOKENS
    full = lambda i: (0, 0)
    chunk = lambda i: (jnp.minimum(i, M_PREP_STEPS - 1), 0)

    def tile(i):
        t = jnp.maximum(i - M_PREP_STEPS, 0)
        return (t // tiles_per_seq, t % tiles_per_seq, 0)

    return pl.pallas_call(
        functools.partial(_mixer_kernel, tiles_per_seq=tiles_per_seq),
        grid=(M_PREP_STEPS + b * tiles_per_seq,),
        in_specs=[
            pl.BlockSpec((None, MIX_TOKENS, d), tile),
            pl.BlockSpec((1, d), full),
            pl.BlockSpec((d // M_PREP_STEPS, d_in), chunk),
            pl.BlockSpec((1, a_width), full),
            pl.BlockSpec(wcat.shape, full),
            pl.BlockSpec(bias.shape, full),
            pl.BlockSpec(w_pool.shape, lambda i: (0, 0, 0)),
            pl.BlockSpec((1, b_width), full),
            pl.BlockSpec((d_mix // M_PREP_STEPS, d), chunk),
        ],
        out_specs=pl.BlockSpec((None, MIX_TOKENS, d), tile),
        out_shape=jax.ShapeDtypeStruct((b, s, d), jnp.float32),
        scratch_shapes=[pltpu.VMEM((d, d_in), jnp.bfloat16),
                        pltpu.VMEM((d_mix, d), jnp.bfloat16),
                        pltpu.VMEM((HALO, b_width), jnp.float32)],
        compiler_params=pltpu.CompilerParams(
            dimension_semantics=("arbitrary",),
            vmem_limit_bytes=V7X_VMEM_LIMIT_BYTES),
        name="mixer",
    )(x3d, gm, w_in, v_gain, wcat, bias, w_pool, p_scale, w_out)


def _ffn_kernel(x_ref, g_ref, wga_ref, wgb_ref, wua_ref, wub_ref, wda_ref, wdb_ref, fg_ref, o_ref,
                wg_s, wu_s, wd_s, h_s, acc_s, *, final_norm):
    i = pl.program_id(0)
    n_chunks = wg_s.shape[0]

    @pl.when(i == 0)
    def _():
        h_s[...] = _rmsnorm(x_ref[...], g_ref[...]).astype(jnp.bfloat16)
        acc_s[...] = jnp.zeros_like(acc_s)

    @pl.when(i < n_chunks)
    def _():
        wg = jnp.concatenate([wga_ref[...], wgb_ref[...]], axis=0).astype(jnp.bfloat16)
        wu = jnp.concatenate([wua_ref[...], wub_ref[...]], axis=0).astype(jnp.bfloat16)
        wd = (0.5 * jnp.concatenate([wda_ref[...], wdb_ref[...]], axis=0)).astype(jnp.bfloat16)
        wg_s[i] = wg
        wu_s[i] = wu
        wd_s[pl.ds(pl.multiple_of(i * FFN_COLS, FFN_COLS), FFN_COLS), :] = wd
        h = h_s[...]
        a = (_silu(_dot(h, wg)) * _dot(h, wu)).astype(jnp.bfloat16)
        acc_s[...] += _dot(a, wd)

    @pl.when(i == n_chunks - 1)
    def _():
        y = x_ref[...] + acc_s[...]
        if final_norm:
            y = _rmsnorm(y, fg_ref[...])
        o_ref[...] = y

    @pl.when(i >= n_chunks)
    def _():
        r = 0
        for rows in FFN_ROW_BLOCKS:
            y = _ffn_rows(x_ref[r:r + rows, :], g_ref, wg_s, wu_s, wd_s)
            if final_norm:
                y = _rmsnorm(y, fg_ref[...])
            o_ref[r:r + rows, :] = y
            r += rows


def _ffn_call(x2d, g, wg, wu, wd, fg, *, final_norm):
    n, d = x2d.shape
    d_ff = wg.shape[1]
    assert n % FFN_TOKENS == 0 and sum(FFN_ROW_BLOCKS) == FFN_TOKENS and d_ff % FFN_COLS == 0
    n_chunks = d_ff // FFN_COLS
    full = lambda i: (0, 0)
    tile = lambda i: (jnp.maximum(i - (n_chunks - 1), 0), 0)
    chunk = lambda i: jnp.minimum(i, n_chunks - 1)
    col_top, col_bot = (lambda i: (0, chunk(i))), (lambda i: (1, chunk(i)))
    row_top, row_bot = (lambda i: (2 * chunk(i), 0)), (lambda i: (2 * chunk(i) + 1, 0))
    return pl.pallas_call(
        functools.partial(_ffn_kernel, final_norm=final_norm),
        grid=(n_chunks - 1 + n // FFN_TOKENS,),
        in_specs=[
            pl.BlockSpec((FFN_TOKENS, d), tile),
            pl.BlockSpec((1, d), full),
            pl.BlockSpec((d // 2, FFN_COLS), col_top),
            pl.BlockSpec((d // 2, FFN_COLS), col_bot),
            pl.BlockSpec((d // 2, FFN_COLS), col_top),
            pl.BlockSpec((d // 2, FFN_COLS), col_bot),
            pl.BlockSpec((FFN_COLS // 2, d), row_top),
            pl.BlockSpec((FFN_COLS // 2, d), row_bot),
            pl.BlockSpec((1, d), full),
        ],
        out_specs=pl.BlockSpec((FFN_TOKENS, d), tile),
        out_shape=jax.ShapeDtypeStruct((n, d), jnp.float32),
        scratch_shapes=[pltpu.VMEM((n_chunks, d, FFN_COLS), jnp.bfloat16),
                        pltpu.VMEM((n_chunks, d, FFN_COLS), jnp.bfloat16),
                        pltpu.VMEM((d_ff, d), jnp.bfloat16),
                        pltpu.VMEM((FFN_TOKENS, d), jnp.bfloat16),
                        pltpu.VMEM((FFN_TOKENS, d), jnp.float32)],
        compiler_params=pltpu.CompilerParams(
            dimension_semantics=("arbitrary",),
            vmem_limit_bytes=V7X_VMEM_LIMIT_BYTES),
        name="ffn_final" if final_norm else "ffn",
    )(x2d, g, wg, wg, wu, wu, wd, wd, fg)


def kernel(x, ffn1_norm, ffn1_w_gate, ffn1_w_up, ffn1_w_down, mix_norm, w_in, gmlp_v_norm,
           gmlp_w_s, gmlp_b_s, pool_w, pool_scale, w_out, ffn2_norm, ffn2_w_gate, ffn2_w_up,
           ffn2_w_down, final_norm):
    bsz, seq, d = x.shape
    depth = ffn1_norm.shape[0]
    fg = final_norm.reshape(1, d)
    for l in range(depth):
        last = l == depth - 1
        x2d = _ffn_call(x.reshape(bsz * seq, d), ffn1_norm[l].reshape(1, d), ffn1_w_gate[l],
                        ffn1_w_up[l], ffn1_w_down[l], fg, final_norm=False)
        wcat = jnp.transpose(gmlp_w_s[l], (1, 0, 2)).reshape(CHUNK, A_HEADS * CHUNK)
        bias = jnp.repeat(gmlp_b_s[l].T, A_HEAD_DIM, axis=1)
        x3d = _mixer_call(x2d.reshape(bsz, seq, d), mix_norm[l].reshape(1, d), w_in[l],
                          gmlp_v_norm[l].reshape(1, -1), wcat, bias, pool_w[l],
                          pool_scale[l].reshape(1, -1), w_out[l])
        x = _ffn_call(x3d.reshape(bsz * seq, d), ffn2_norm[l].reshape(1, d), ffn2_w_gate[l],
                      ffn2_w_up[l], ffn2_w_down[l], fg, final_norm=last).reshape(bsz, seq, d)
    return x
```

```python
import functools

import jax
import jax.numpy as jnp
from jax import lax
from jax.experimental import pallas as pl
from jax.experimental.pallas import tpu as pltpu

EPS = 1e-6
CHUNK = 128
A_HEADS = 8
A_HEAD_DIM = 64
POOL_WINDOWS = (2, 4, 8, 16)
POOL_GROUP_DIM = 128
HALO = 16
assert all(w & (w - 1) == 0 and w <= HALO for w in POOL_WINDOWS)

V7X_MXU_DIM = 256
V7X_VMEM_LIMIT_BYTES = 56 * 1024 * 1024
BF16_SUBLANES = 16

FFN_TOKENS = 1024
MIX_TOKENS = 1024
FFN_ROW_BLOCKS = (256, 256, 256, 256)
MIX_ROWS = 512
FFN_COLS = 256
M_PREP_STEPS = 2


def _rmsnorm(x, g):
    r = lax.rsqrt(jnp.mean(x * x, axis=-1, keepdims=True) + EPS)
    return (x * r) * g


def _silu(x):
    return x * (0.5 * jnp.tanh(0.5 * x) + 0.5)


def _gelu_tanh(x):
    c = 0.7978845608028654
    return 0.5 * x * (1.0 + jnp.tanh(c * (x + 0.044715 * (x * x * x))))


def _dot(a, b):
    return jnp.dot(a, b, preferred_element_type=jnp.float32)


def _store_bf16_chunk(i, src_ref, dst_ref):
    rows = src_ref.shape[0]
    r0 = pl.multiple_of(i * rows, rows)
    dst_ref[pl.ds(r0, rows), :] = src_ref[...].astype(jnp.bfloat16)


def _ffn_rows(x, g_ref, wg_s, wu_s, wd_s):
    h = _rmsnorm(x, g_ref[...]).astype(jnp.bfloat16)
    acts = []
    for c in range(wg_s.shape[0]):
        gate = _dot(h, wg_s[c])
        up = _dot(h, wu_s[c])
        acts.append((_silu(gate) * up).astype(jnp.bfloat16))
    a = jnp.concatenate(acts, axis=1)
    return x + _dot(a, wd_s[...])


def _mixer_rows(x, halo, first_pos, consts, g_ref, win_s, vg_ref, bias_ref, wout_s):
    seg, wcat, head_masks = consts
    rows = x.shape[0]
    a_width = A_HEADS * A_HEAD_DIM
    heads_per_group = len(head_masks)

    h = _rmsnorm(x, g_ref[...]).astype(jnp.bfloat16)
    p = _dot(h, win_s[...])
    u = _gelu_tanh(p[:, :a_width])
    v = _gelu_tanh(p[:, a_width:2 * a_width])
    z = p[:, 2 * a_width:]

    vv = (v * v).astype(jnp.bfloat16)
    ms = jnp.concatenate(
        [_dot(vv[:, gs:gs + V7X_MXU_DIM], seg) for gs in range(0, a_width, V7X_MXU_DIM)],
        axis=1) * (1.0 / A_HEAD_DIM)
    vn = ((v * lax.rsqrt(ms + EPS)) * vg_ref[...]).astype(jnp.bfloat16)

    ya_chunks = []
    for c in range(0, rows, CHUNK):
        groups = []
        for gi in range(a_width // V7X_MXU_DIM):
            vg = vn[c:c + CHUNK, gi * V7X_MXU_DIM:(gi + 1) * V7X_MXU_DIM]
            stack = jnp.concatenate([vg * m for m in head_masks], axis=0)
            k0 = gi * heads_per_group * CHUNK
            groups.append(_dot(wcat[:, k0:k0 + heads_per_group * CHUNK], stack))
        mixed = jnp.concatenate(groups, axis=1) + bias_ref[...]
        ya_chunks.append(u[c:c + CHUNK, :] * mixed)
    y_a = jnp.concatenate(ya_chunks, axis=0)

    zext = jnp.concatenate([halo, z], axis=0)
    head_pos = first_pos + lax.broadcasted_iota(jnp.int32, (HALO, POOL_GROUP_DIM), 0)
    d_groups = []
    for gi, win in enumerate(POOL_WINDOWS):
        lanes = slice(gi * POOL_GROUP_DIM, (gi + 1) * POOL_GROUP_DIM)
        s = zext[:, lanes]
        step = 1
        while step < win:
            s = s + pltpu.roll(s, step, 0)
            step *= 2
        head_count = jnp.minimum(head_pos, win).astype(jnp.float32)
        mean = jnp.concatenate([s[HALO:2 * HALO, :] / head_count,
                                s[2 * HALO:, :] * (1.0 / win)], axis=0)
        d_groups.append((mean - z[:, lanes]).astype(jnp.bfloat16))

    y = jnp.concatenate([y_a.astype(jnp.bfloat16)] + d_groups, axis=1)
    return x + _dot(y, wout_s[...]), z[rows - HALO:, :]


def _prep_wout(i, n_steps, wout_ref, wp_ref, ps_ref, wout_s):
    a_width = A_HEADS * A_HEAD_DIM
    chunk_rows = wout_ref.shape[0]
    for step in range(n_steps):
        @pl.when(i == step)
        def _(step=step):
            r0 = step * chunk_rows
            if r0 < a_width:
                wout_s[r0:r0 + chunk_rows, :] = wout_ref[...].astype(jnp.bfloat16)
            else:
                for lr in range(0, chunk_rows, POOL_GROUP_DIM):
                    g = (r0 + lr - a_width) // POOL_GROUP_DIM
                    lanes = slice(g * POOL_GROUP_DIM, (g + 1) * POOL_GROUP_DIM)
                    folded = jnp.dot(wp_ref[g] * ps_ref[:, lanes], wout_ref[lr:lr + POOL_GROUP_DIM, :],
                                     precision=lax.Precision.HIGHEST,
                                     preferred_element_type=jnp.float32)
                    wout_s[r0 + lr:r0 + lr + POOL_GROUP_DIM, :] = folded.astype(jnp.bfloat16)


def _mixer_consts(wcat_ref):
    row_head = lax.broadcasted_iota(jnp.int32, (V7X_MXU_DIM, V7X_MXU_DIM), 0) // A_HEAD_DIM
    col_head = lax.broadcasted_iota(jnp.int32, (V7X_MXU_DIM, V7X_MXU_DIM), 1) // A_HEAD_DIM
    seg = (row_head == col_head).astype(jnp.bfloat16)
    wrow = lax.broadcasted_iota(jnp.int32, wcat_ref.shape, 0)
    wcol = lax.broadcasted_iota(jnp.int32, wcat_ref.shape, 1) % CHUNK
    wcat = jnp.where(wcol <= wrow, wcat_ref[...], 0.0).astype(jnp.bfloat16)
    lane_head = lax.broadcasted_iota(jnp.int32, (CHUNK, V7X_MXU_DIM), 1) // A_HEAD_DIM
    head_masks = [(lane_head == hh).astype(jnp.bfloat16) for hh in range(V7X_MXU_DIM // A_HEAD_DIM)]
    return seg, wcat, head_masks


def _mixer_kernel(x_ref, gm_ref, win_ref, vg_ref, wcat_ref, bias_ref, wp_ref, ps_ref, wout_ref,
                  o_ref, win_s, wout_s, zprev_ref, *, tiles_per_seq):
    i = pl.program_id(0)

    @pl.when(i < M_PREP_STEPS)
    def _():
        _store_bf16_chunk(i, win_ref, win_s)

    _prep_wout(i, M_PREP_STEPS, wout_ref, wp_ref, ps_ref, wout_s)

    @pl.when(i >= M_PREP_STEPS)
    def _():
        tm = x_ref.shape[0]
        j = (i - M_PREP_STEPS) % tiles_per_seq
        consts = _mixer_consts(wcat_ref)

        @pl.when(j == 0)
        def _():
            zprev_ref[...] = jnp.zeros_like(zprev_ref)

        halo = zprev_ref[...]
        for r in range(0, tm, MIX_ROWS):
            x2, halo = _mixer_rows(x_ref[r:r + MIX_ROWS, :], halo, j * tm + (r + 1), consts,
                                   gm_ref, win_s, vg_ref, bias_ref, wout_s)
            o_ref[r:r + MIX_ROWS, :] = x2
        zprev_ref[...] = halo


def _mixer_call(x3d, gm, w_in, v_gain, wcat, bias, w_pool, p_scale, w_out):
    b, s, d = x3d.shape
    d_in, d_mix = w_in.shape[1], w_out.shape[0]
    a_width = A_HEADS * A_HEAD_DIM
    b_width = len(POOL_WINDOWS) * POOL_GROUP_DIM
    assert s % MIX_TOKENS == 0 and MIX_TOKENS % MIX_ROWS == 0 and MIX_ROWS % CHUNK == 0
    assert d_in == 2 * a_width + b_width and d_mix == a_width + b_width
    assert d % (M_PREP_STEPS * BF16_SUBLANES) == 0 and d_mix % (M_PREP_STEPS * BF16_SUBLANES) == 0
    assert (d_mix // M_PREP_STEPS) % POOL_GROUP_DIM == 0 and a_width % (d_mix // M_PREP_STEPS) == 0
    tiles_per_seq = s // MIX_TOKENS
    full = lambda i: (0, 0)
    chunk = lambda i: (jnp.minimum(i, M_PREP_STEPS - 1), 0)

    def tile(i):
        t = jnp.maximum(i - M_PREP_STEPS, 0)
        return (t // tiles_per_seq, t % tiles_per_seq, 0)

    return pl.pallas_call(
        functools.partial(_mixer_kernel, tiles_per_seq=tiles_per_seq),
        grid=(M_PREP_STEPS + b * tiles_per_seq,),
        in_specs=[
            pl.BlockSpec((None, MIX_TOKENS, d), tile),
            pl.BlockSpec((1, d), full),
            pl.BlockSpec((d // M_PREP_STEPS, d_in), chunk),
            pl.BlockSpec((1, a_width), full),
            pl.BlockSpec(wcat.shape, full),
            pl.BlockSpec(bias.shape, full),
            pl.BlockSpec(w_pool.shape, lambda i: (0, 0, 0)),
            pl.BlockSpec((1, b_width), full),
            pl.BlockSpec((d_mix // M_PREP_STEPS, d), chunk),
        ],
        out_specs=pl.BlockSpec((None, MIX_TOKENS, d), tile),
        out_shape=jax.ShapeDtypeStruct((b, s, d), jnp.float32),
        scratch_shapes=[pltpu.VMEM((d, d_in), jnp.bfloat16),
                        pltpu.VMEM((d_mix, d), jnp.bfloat16),
                        pltpu.VMEM((HALO, b_width), jnp.float32)],
        compiler_params=pltpu.CompilerParams(
            dimension_semantics=("arbitrary",),
            vmem_limit_bytes=V7X_VMEM_LIMIT_BYTES),
        name="mixer",
    )(x3d, gm, w_in, v_gain, wcat, bias, w_pool, p_scale, w_out)


def _ffn_kernel(x_ref, g_ref, wga_ref, wgb_ref, wua_ref, wub_ref, wda_ref, wdb_ref, fg_ref, o_ref,
                wg_s, wu_s, wd_s, h_s, acc_s, *, final_norm):
    i = pl.program_id(0)
    n_chunks = wg_s.shape[0]

    @pl.when(i == 0)
    def _():
        h_s[...] = _rmsnorm(x_ref[...], g_ref[...]).astype(jnp.bfloat16)
        acc_s[...] = jnp.zeros_like(acc_s)

    @pl.when(i < n_chunks)
    def _():
        wg = jnp.concatenate([wga_ref[...], wgb_ref[...]], axis=0).astype(jnp.bfloat16)
        wu = jnp.concatenate([wua_ref[...], wub_ref[...]], axis=0).astype(jnp.bfloat16)
        wd = (0.5 * jnp.concatenate([wda_ref[...], wdb_ref[...]], axis=0)).astype(jnp.bfloat16)
        wg_s[i] = wg
        wu_s[i] = wu
        wd_s[pl.ds(pl.multiple_of(i * FFN_COLS, FFN_COLS), FFN_COLS), :] = wd
        h = h_s[...]
        a = (_silu(_dot(h, wg)) * _dot(h, wu)).astype(jnp.bfloat16)
        acc_s[...] += _dot(a, wd)

    @pl.when(i == n_chunks - 1)
    def _():
        y = x_ref[...] + acc_s[...]
        if final_norm:
            y = _rmsnorm(y, fg_ref[...])
        o_ref[...] = y

    @pl.when(i >= n_chunks)
    def _():
        r = 0
        for rows in FFN_ROW_BLOCKS:
            y = _ffn_rows(x_ref[r:r + rows, :], g_ref, wg_s, wu_s, wd_s)
            if final_norm:
                y = _rmsnorm(y, fg_ref[...])
            o_ref[r:r + rows, :] = y
            r += rows


def _ffn_call(x2d, g, wg, wu, wd, fg, *, final_norm):
    n, d = x2d.shape
    d_ff = wg.shape[1]
    assert n % FFN_TOKENS == 0 and sum(FFN_ROW_BLOCKS) == FFN_TOKENS and d_ff % FFN_COLS == 0
    n_chunks = d_ff // FFN_COLS
    full = lambda i: (0, 0)
    tile = lambda i: (jnp.maximum(i - (n_chunks - 1), 0), 0)
    chunk = lambda i: jnp.minimum(i, n_chunks - 1)
    col_top, col_bot = (lambda i: (0, chunk(i))), (lambda i: (1, chunk(i)))
    row_top, row_bot = (lambda i: (2 * chunk(i), 0)), (lambda i: (2 * chunk(i) + 1, 0))
    return pl.pallas_call(
        functools.partial(_ffn_kernel, final_norm=final_norm),
        grid=(n_chunks - 1 + n // FFN_TOKENS,),
        in_specs=[
            pl.BlockSpec((FFN_TOKENS, d), tile),
            pl.BlockSpec((1, d), full),
            pl.BlockSpec((d // 2, FFN_COLS), col_top),
            pl.BlockSpec((d // 2, FFN_COLS), col_bot),
            pl.BlockSpec((d // 2, FFN_COLS), col_top),
            pl.BlockSpec((d // 2, FFN_COLS), col_bot),
            pl.BlockSpec((FFN_COLS // 2, d), row_top),
            pl.BlockSpec((FFN_COLS // 2, d), row_bot),
            pl.BlockSpec((1, d), full),
        ],
        out_specs=pl.BlockSpec((FFN_TOKENS, d), tile),
        out_shape=jax.ShapeDtypeStruct((n, d), jnp.float32),
        scratch_shapes=[pltpu.VMEM((n_chunks, d, FFN_COLS), jnp.bfloat16),
                        pltpu.VMEM((n_chunks, d, FFN_COLS), jnp.bfloat16),
                        pltpu.VMEM((d_ff, d), jnp.bfloat16),
                        pltpu.VMEM((FFN_TOKENS, d), jnp.bfloat16),
                        pltpu.VMEM((FFN_TOKENS, d), jnp.float32)],
        compiler_params=pltpu.CompilerParams(
            dimension_semantics=("arbitrary",),
            vmem_limit_bytes=V7X_VMEM_LIMIT_BYTES),
        name="ffn_final" if final_norm else "ffn",
    )(x2d, g, wg, wg, wu, wu, wd, wd, fg)


def kernel(x, ffn1_norm, ffn1_w_gate, ffn1_w_up, ffn1_w_down, mix_norm, w_in, gmlp_v_norm,
           gmlp_w_s, gmlp_b_s, pool_w, pool_scale, w_out, ffn2_norm, ffn2_w_gate, ffn2_w_up,
           ffn2_w_down, final_norm):
    bsz, seq, d = x.shape
    depth = ffn1_norm.shape[0]
    fg = final_norm.reshape(1, d)
    for l in range(depth):
        last = l == depth - 1
        x2d = _ffn_call(x.reshape(bsz * seq, d), ffn1_norm[l].reshape(1, d), ffn1_w_gate[l],
                        ffn1_w_up[l], ffn1_w_down[l], fg, final_norm=False)
        wcat = jnp.transpose(gmlp_w_s[l], (1, 0, 2)).reshape(CHUNK, A_HEADS * CHUNK)
        bias = jnp.repeat(gmlp_b_s[l].T, A_HEAD_DIM, axis=1)
        x3d = _mixer_call(x2d.reshape(bsz, seq, d), mix_norm[l].reshape(1, d), w_in[l],
                          gmlp_v_norm[l].reshape(1, -1), wcat, bias, pool_w[l],
                          pool_scale[l].reshape(1, -1), w_out[l])
        x = _ffn_call(x3d.reshape(bsz * seq, d), ffn2_norm[l].reshape(1, d), ffn2_w_gate[l],
                      ffn2_w_up[l], ffn2_w_down[l], fg, final_norm=last).reshape(bsz, seq, d)
    return x
```

```python
import functools

import jax
import jax.numpy as jnp
from jax import lax
from jax.experimental import pallas as pl
from jax.experimental.pallas import tpu as pltpu

EPS = 1e-6
CHUNK = 128
A_HEADS = 8
A_HEAD_DIM = 64
POOL_WINDOWS = (2, 4, 8, 16)
POOL_GROUP_DIM = 128
HALO = 16
assert all(w & (w - 1) == 0 and w <= HALO for w in POOL_WINDOWS)

V7X_MXU_DIM = 256
V7X_VMEM_LIMIT_BYTES = 56 * 1024 * 1024
BF16_SUBLANES = 16

FFN_TOKENS = 1024
MIX_TOKENS = 1024
FFN_ROW_BLOCKS = (256, 256, 256, 256)
MIX_ROWS = 512
FFN_COLS = 256
M_PREP_STEPS = 2


def _rmsnorm(x, g):
    r = lax.rsqrt(jnp.mean(x * x, axis=-1, keepdims=True) + EPS)
    return (x * r) * g


def _silu(x):
    return x * (0.5 * jnp.tanh(0.5 * x) + 0.5)


def _gelu_tanh(x):
    c = 0.7978845608028654
    return 0.5 * x * (1.0 + jnp.tanh(c * (x + 0.044715 * (x * x * x))))


def _dot(a, b):
    return jnp.dot(a, b, preferred_element_type=jnp.float32)


def _store_bf16_chunk(i, src_ref, dst_ref):
    rows = src_ref.shape[0]
    r0 = pl.multiple_of(i * rows, rows)
    dst_ref[pl.ds(r0, rows), :] = src_ref[...].astype(jnp.bfloat16)


def _ffn_rows(x, g_ref, wg_s, wu_s, wd_s):
    h = _rmsnorm(x, g_ref[...]).astype(jnp.bfloat16)
    acts = []
    for c in range(wg_s.shape[0]):
        gate = _dot(h, wg_s[c])
        up = _dot(h, wu_s[c])
        acts.append((_silu(gate) * up).astype(jnp.bfloat16))
    a = jnp.concatenate(acts, axis=1)
    return x + _dot(a, wd_s[...])


def _mixer_rows(x, halo, first_pos, consts, g_ref, win_s, vg_ref, bias_ref, wout_s):
    seg, wcat, head_masks = consts
    rows = x.shape[0]
    a_width = A_HEADS * A_HEAD_DIM
    heads_per_group = len(head_masks)

    h = _rmsnorm(x, g_ref[...]).astype(jnp.bfloat16)
    p = _dot(h, win_s[...])
    u = _gelu_tanh(p[:, :a_width])
    v = _gelu_tanh(p[:, a_width:2 * a_width])
    z = p[:, 2 * a_width:]

    vv = (v * v).astype(jnp.bfloat16)
    ms = jnp.concatenate(
        [_dot(vv[:, gs:gs + V7X_MXU_DIM], seg) for gs in range(0, a_width, V7X_MXU_DIM)],
        axis=1) * (1.0 / A_HEAD_DIM)
    vn = ((v * lax.rsqrt(ms + EPS)) * vg_ref[...]).astype(jnp.bfloat16)

    ya_chunks = []
    for c in range(0, rows, CHUNK):
        groups = []
        for gi in range(a_width // V7X_MXU_DIM):
            vg = vn[c:c + CHUNK, gi * V7X_MXU_DIM:(gi + 1) * V7X_MXU_DIM]
            stack = jnp.concatenate([vg * m for m in head_masks], axis=0)
            k0 = gi * heads_per_group * CHUNK
            groups.append(_dot(wcat[:, k0:k0 + heads_per_group * CHUNK], stack))
        mixed = jnp.concatenate(groups, axis=1) + bias_ref[...]
        ya_chunks.append(u[c:c + CHUNK, :] * mixed)
    y_a = jnp.concatenate(ya_chunks, axis=0)

    zext = jnp.concatenate([halo, z], axis=0)
    head_pos = first_pos + lax.broadcasted_iota(jnp.int32, (HALO, POOL_GROUP_DIM), 0)
    d_groups = []
    for gi, win in enumerate(POOL_WINDOWS):
        lanes = slice(gi * POOL_GROUP_DIM, (gi + 1) * POOL_GROUP_DIM)
        s = zext[:, lanes]
        step = 1
        while step < win:
            s = s + pltpu.roll(s, step, 0)
            step *= 2
        head_count = jnp.minimum(head_pos, win).astype(jnp.float32)
        mean = jnp.concatenate([s[HALO:2 * HALO, :] / head_count,
                                s[2 * HALO:, :] * (1.0 / win)], axis=0)
        d_groups.append((mean - z[:, lanes]).astype(jnp.bfloat16))

    y = jnp.concatenate([y_a.astype(jnp.bfloat16)] + d_groups, axis=1)
    return x + _dot(y, wout_s[...]), z[rows - HALO:, :]


def _prep_wout(i, n_steps, wout_ref, wp_ref, ps_ref, wout_s):
    a_width = A_HEADS * A_HEAD_DIM
    chunk_rows = wout_ref.shape[0]
    for step in range(n_steps):
        @pl.when(i == step)
        def _(step=step):
            r0 = step * chunk_rows
            if r0 < a_width:
                wout_s[r0:r0 + chunk_rows, :] = wout_ref[...].astype(jnp.bfloat16)
            else:
                for lr in range(0, chunk_rows, POOL_GROUP_DIM):
                    g = (r0 + lr - a_width) // POOL_GROUP_DIM
                    lanes = slice(g * POOL_GROUP_DIM, (g + 1) * POOL_GROUP_DIM)
                    folded = jnp.dot(wp_ref[g] * ps_ref[:, lanes], wout_ref[lr:lr + POOL_GROUP_DIM, :],
                                     precision=lax.Precision.HIGHEST,
                                     preferred_element_type=jnp.float32)
                    wout_s[r0 + lr:r0 + lr + POOL_GROUP_DIM, :] = folded.astype(jnp.bfloat16)


def _mixer_consts(wcat_ref):
    row_head = lax.broadcasted_iota(jnp.int32, (V7X_MXU_DIM, V7X_MXU_DIM), 0) // A_HEAD_DIM
    col_head = lax.broadcasted_iota(jnp.int32, (V7X_MXU_DIM, V7X_MXU_DIM), 1) // A_HEAD_DIM
    seg = (row_head == col_head).astype(jnp.bfloat16)
    wrow = lax.broadcasted_iota(jnp.int32, wcat_ref.shape, 0)
    wcol = lax.broadcasted_iota(jnp.int32, wcat_ref.shape, 1) % CHUNK
    wcat = jnp.where(wcol <= wrow, wcat_ref[...], 0.0).astype(jnp.bfloat16)
    lane_head = lax.broadcasted_iota(jnp.int32, (CHUNK, V7X_MXU_DIM), 1) // A_HEAD_DIM
    head_masks = [(lane_head == hh).astype(jnp.bfloat16) for hh in range(V7X_MXU_DIM // A_HEAD_DIM)]
    return seg, wcat, head_masks


def _mixer_kernel(x_ref, gm_ref, win_ref, vg_ref, wcat_ref, bias_ref, wp_ref, ps_ref, wout_ref,
                  o_ref, win_s, wout_s, zprev_ref, *, tiles_per_seq):
    i = pl.program_id(0)

    @pl.when(i < M_PREP_STEPS)
    def _():
        _store_bf16_chunk(i, win_ref, win_s)

    _prep_wout(i, M_PREP_STEPS, wout_ref, wp_ref, ps_ref, wout_s)

    @pl.when(i >= M_PREP_STEPS)
    def _():
        tm = x_ref.shape[0]
        j = (i - M_PREP_STEPS) % tiles_per_seq
        consts = _mixer_consts(wcat_ref)

        @pl.when(j == 0)
        def _():
            zprev_ref[...] = jnp.zeros_like(zprev_ref)

        halo = zprev_ref[...]
        for r in range(0, tm, MIX_ROWS):
            x2, halo = _mixer_rows(x_ref[r:r + MIX_ROWS, :], halo, j * tm + (r + 1), consts,
                                   gm_ref, win_s, vg_ref, bias_ref, wout_s)
            o_ref[r:r + MIX_ROWS, :] = x2
        zprev_ref[...] = halo


def _mixer_call(x3d, gm, w_in, v_gain, wcat, bias, w_pool, p_scale, w_out):
    b, s, d = x3d.shape
    d_in, d_mix = w_in.shape[1], w_out.shape[0]
    a_width = A_HEADS * A_HEAD_DIM
    b_width = len(POOL_WINDOWS) * POOL_GROUP_DIM
    assert s % MIX_TOKENS == 0 and MIX_TOKENS % MIX_ROWS == 0 and MIX_ROWS % CHUNK == 0
    assert d_in == 2 * a_width + b_width and d_mix == a_width + b_width
    assert d % (M_PREP_STEPS * BF16_SUBLANES) == 0 and d_mix % (M_PREP_STEPS * BF16_SUBLANES) == 0
    assert (d_mix // M_PREP_STEPS) % POOL_GROUP_DIM == 0 and a_width % (d_mix // M_PREP_STEPS) == 0
    tiles_per_seq = s // MIX_TOKENS
    full = lambda i: (0, 0)
    chunk = lambda i: (jnp.minimum(i, M_PREP_STEPS - 1), 0)

    def tile(i):
        t = jnp.maximum(i - M_PREP_STEPS, 0)
        return (t // tiles_per_seq, t % tiles_per_seq, 0)

    return pl.pallas_call(
        functools.partial(_mixer_kernel, tiles_per_seq=tiles_per_seq),
        grid=(M_PREP_STEPS + b * tiles_per_seq,),
        in_specs=[
            pl.BlockSpec((None, MIX_TOKENS, d), tile),
            pl.BlockSpec((1, d), full),
            pl.BlockSpec((d // M_PREP_STEPS, d_in), chunk),
            pl.BlockSpec((1, a_width), full),
            pl.BlockSpec(wcat.shape, full),
            pl.BlockSpec(bias.shape, full),
            pl.BlockSpec(w_pool.shape, lambda i: (0, 0, 0)),
            pl.BlockSpec((1, b_width), full),
            pl.BlockSpec((d_mix // M_PREP_STEPS, d), chunk),
        ],
        out_specs=pl.BlockSpec((None, MIX_TOKENS, d), tile),
        out_shape=jax.ShapeDtypeStruct((b, s, d), jnp.float32),
        scratch_shapes=[pltpu.VMEM((d, d_in), jnp.bfloat16),
                        pltpu.VMEM((d_mix, d), jnp.bfloat16),
                        pltpu.VMEM((HALO, b_width), jnp.float32)],
        compiler_params=pltpu.CompilerParams(
            dimension_semantics=("arbitrary",),
            vmem_limit_bytes=V7X_VMEM_LIMIT_BYTES),
        name="mixer",
    )(x3d, gm, w_in, v_gain, wcat, bias, w_pool, p_scale, w_out)


def _ffn_kernel(x_ref, g_ref, wga_ref, wgb_ref, wua_ref, wub_ref, wda_ref, wdb_ref, fg_ref, o_ref,
                wg_s, wu_s, wd_s, h_s, a_s, *, final_norm):
    i = pl.program_id(0)
    n_chunks = wg_s.shape[0]

    @pl.when(i == 0)
    def _():
        h_s[...] = _rmsnorm(x_ref[...], g_ref[...]).astype(jnp.bfloat16)

    @pl.when(i < n_chunks)
    def _():
        wg = jnp.concatenate([wga_ref[...], wgb_ref[...]], axis=0).astype(jnp.bfloat16)
        wu = jnp.concatenate([wua_ref[...], wub_ref[...]], axis=0).astype(jnp.bfloat16)
        wd = (0.5 * jnp.concatenate([wda_ref[...], wdb_ref[...]], axis=0)).astype(jnp.bfloat16)
        wg_s[i] = wg
        wu_s[i] = wu
        wd_s[pl.ds(pl.multiple_of(i * FFN_COLS, FFN_COLS), FFN_COLS), :] = wd
        h = h_s[...]
        a_s[i] = (_silu(_dot(h, wg)) * _dot(h, wu)).astype(jnp.bfloat16)

    @pl.when(i == n_chunks - 1)
    def _():
        a = jnp.concatenate([a_s[c] for c in range(n_chunks)], axis=1)
        y = x_ref[...] + _dot(a, wd_s[...])
        if final_norm:
            y = _rmsnorm(y, fg_ref[...])
        o_ref[...] = y

    @pl.when(i >= n_chunks)
    def _():
        r = 0
        for rows in FFN_ROW_BLOCKS:
            y = _ffn_rows(x_ref[r:r + rows, :], g_ref, wg_s, wu_s, wd_s)
            if final_norm:
                y = _rmsnorm(y, fg_ref[...])
            o_ref[r:r + rows, :] = y
            r += rows


def _ffn_call(x2d, g, wg, wu, wd, fg, *, final_norm):
    n, d = x2d.shape
    d_ff = wg.shape[1]
    assert n % FFN_TOKENS == 0 and sum(FFN_ROW_BLOCKS) == FFN_TOKENS and d_ff % FFN_COLS == 0
    n_chunks = d_ff // FFN_COLS
    full = lambda i: (0, 0)
    tile = lambda i: (jnp.maximum(i - (n_chunks - 1), 0), 0)
    chunk = lambda i: jnp.minimum(i, n_chunks - 1)
    col_top, col_bot = (lambda i: (0, chunk(i))), (lambda i: (1, chunk(i)))
    row_top, row_bot = (lambda i: (2 * chunk(i), 0)), (lambda i: (2 * chunk(i) + 1, 0))
    return pl.pallas_call(
        functools.partial(_ffn_kernel, final_norm=final_norm),
        grid=(n_chunks - 1 + n // FFN_TOKENS,),
        in_specs=[
            pl.BlockSpec((FFN_TOKENS, d), tile),
            pl.BlockSpec((1, d), full),
            pl.BlockSpec((d // 2, FFN_COLS), col_top),
            pl.BlockSpec((d // 2, FFN_COLS), col_bot),
            pl.BlockSpec((d // 2, FFN_COLS), col_top),
            pl.BlockSpec((d // 2, FFN_COLS), col_bot),
            pl.BlockSpec((FFN_COLS // 2, d), row_top),
            pl.BlockSpec((FFN_COLS // 2, d), row_bot),
            pl.BlockSpec((1, d), full),
        ],
        out_specs=pl.BlockSpec((FFN_TOKENS, d), tile),
        out_shape=jax.ShapeDtypeStruct((n, d), jnp.float32),
        scratch_shapes=[pltpu.VMEM((n_chunks, d, FFN_COLS), jnp.bfloat16),
                        pltpu.VMEM((n_chunks, d, FFN_COLS), jnp.bfloat16),
                        pltpu.VMEM((d_ff, d), jnp.bfloat16),
                        pltpu.VMEM((FFN_TOKENS, d), jnp.bfloat16),
                        pltpu.VMEM((n_chunks, FFN_TOKENS, FFN_COLS), jnp.bfloat16)],
        compiler_params=pltpu.CompilerParams(
            dimension_semantics=("arbitrary",),
            vmem_limit_bytes=V7X_VMEM_LIMIT_BYTES),
        name="ffn_final" if final_norm else "ffn",
    )(x2d, g, wg, wg, wu, wu, wd, wd, fg)


def kernel(x, ffn1_norm, ffn1_w_gate, ffn1_w_up, ffn1_w_down, mix_norm, w_in, gmlp_v_norm,
           gmlp_w_s, gmlp_b_s, pool_w, pool_scale, w_out, ffn2_norm, ffn2_w_gate, ffn2_w_up,
           ffn2_w_down, final_norm):
    bsz, seq, d = x.shape
    depth = ffn1_norm.shape[0]
    fg = final_norm.reshape(1, d)
    for l in range(depth):
        last = l == depth - 1
        x2d = _ffn_call(x.reshape(bsz * seq, d), ffn1_norm[l].reshape(1, d), ffn1_w_gate[l],
                        ffn1_w_up[l], ffn1_w_down[l], fg, final_norm=False)
        wcat = jnp.transpose(gmlp_w_s[l], (1, 0, 2)).reshape(CHUNK, A_HEADS * CHUNK)
        bias = jnp.repeat(gmlp_b_s[l].T, A_HEAD_DIM, axis=1)
        x3d = _mixer_call(x2d.reshape(bsz, seq, d), mix_norm[l].reshape(1, d), w_in[l],
                          gmlp_v_norm[l].reshape(1, -1), wcat, bias, pool_w[l],
                          pool_scale[l].reshape(1, -1), w_out[l])
        x = _ffn_call(x3d.reshape(bsz * seq, d), ffn2_norm[l].reshape(1, d), ffn2_w_gate[l],
                      ffn2_w_up[l], ffn2_w_down[l], fg, final_norm=last).reshape(bsz, seq, d)
    return x
```

```python
import functools

import jax
import jax.numpy as jnp
from jax import lax
from jax.experimental import pallas as pl
from jax.experimental.pallas import tpu as pltpu

EPS = 1e-6
CHUNK = 128
A_HEADS = 8
A_HEAD_DIM = 64
POOL_WINDOWS = (2, 4, 8, 16)
POOL_GROUP_DIM = 128
HALO = 16
assert all(w & (w - 1) == 0 and w <= HALO for w in POOL_WINDOWS)

V7X_MXU_DIM = 256
V7X_VMEM_LIMIT_BYTES = 56 * 1024 * 1024
BF16_SUBLANES = 16

FFN_TOKENS = 1024
MIX_TOKENS = 1024
FFN_ROW_BLOCKS = (256, 256, 256, 256)
MIX_ROWS = 512
FFN_COLS = 256
M_PREP_STEPS = 2


def _rmsnorm(x, g):
    r = lax.rsqrt(jnp.mean(x * x, axis=-1, keepdims=True) + EPS)
    return (x * r) * g


def _silu(x):
    return x * (0.5 * jnp.tanh(0.5 * x) + 0.5)


def _gelu_tanh(x):
    c = 0.7978845608028654
    return 0.5 * x * (1.0 + jnp.tanh(c * (x + 0.044715 * (x * x * x))))


def _dot(a, b):
    return jnp.dot(a, b, preferred_element_type=jnp.float32)


def _store_bf16_chunk(i, src_ref, dst_ref):
    rows = src_ref.shape[0]
    r0 = pl.multiple_of(i * rows, rows)
    dst_ref[pl.ds(r0, rows), :] = src_ref[...].astype(jnp.bfloat16)


def _ffn_rows(x, g_ref, wg_s, wu_s, wd_s):
    h = _rmsnorm(x, g_ref[...]).astype(jnp.bfloat16)
    acts = []
    for c in range(wg_s.shape[0]):
        gate = _dot(h, wg_s[c])
        up = _dot(h, wu_s[c])
        acts.append((_silu(gate) * up).astype(jnp.bfloat16))
    a = jnp.concatenate(acts, axis=1)
    return x + _dot(a, wd_s[...])


def _mixer_rows(x, halo, first_pos, consts, g_ref, win_s, vg_ref, bias_ref, wout_s):
    seg, wcat, head_masks = consts
    rows = x.shape[0]
    a_width = A_HEADS * A_HEAD_DIM
    heads_per_group = len(head_masks)

    h = _rmsnorm(x, g_ref[...]).astype(jnp.bfloat16)
    p = _dot(h, win_s[...])
    u = _gelu_tanh(p[:, :a_width])
    v = _gelu_tanh(p[:, a_width:2 * a_width])
    z = p[:, 2 * a_width:]

    vv = (v * v).astype(jnp.bfloat16)
    ms = jnp.concatenate(
        [_dot(vv[:, gs:gs + V7X_MXU_DIM], seg) for gs in range(0, a_width, V7X_MXU_DIM)],
        axis=1) * (1.0 / A_HEAD_DIM)
    vn = ((v * lax.rsqrt(ms + EPS)) * vg_ref[...]).astype(jnp.bfloat16)

    ya_chunks = []
    for c in range(0, rows, CHUNK):
        groups = []
        for gi in range(a_width // V7X_MXU_DIM):
            vg = vn[c:c + CHUNK, gi * V7X_MXU_DIM:(gi + 1) * V7X_MXU_DIM]
            stack = jnp.concatenate([vg * m for m in head_masks], axis=0)
            k0 = gi * heads_per_group * CHUNK
            groups.append(_dot(wcat[:, k0:k0 + heads_per_group * CHUNK], stack))
        mixed = jnp.concatenate(groups, axis=1) + bias_ref[...]
        ya_chunks.append(u[c:c + CHUNK, :] * mixed)
    y_a = jnp.concatenate(ya_chunks, axis=0)

    zext = jnp.concatenate([halo, z], axis=0)
    head_pos = first_pos + lax.broadcasted_iota(jnp.int32, (HALO, POOL_GROUP_DIM), 0)
    d_groups = []
    for gi, win in enumerate(POOL_WINDOWS):
        lanes = slice(gi * POOL_GROUP_DIM, (gi + 1) * POOL_GROUP_DIM)
        s = zext[:, lanes]
        step = 1
        while step < win:
            s = s + pltpu.roll(s, step, 0)
            step *= 2
        head_count = jnp.minimum(head_pos, win).astype(jnp.float32)
        mean = jnp.concatenate([s[HALO:2 * HALO, :] / head_count,
                                s[2 * HALO:, :] * (1.0 / win)], axis=0)
        d_groups.append((mean - z[:, lanes]).astype(jnp.bfloat16))

    y = jnp.concatenate([y_a.astype(jnp.bfloat16)] + d_groups, axis=1)
    return x + _dot(y, wout_s[...]), z[rows - HALO:, :]


def _prep_wout(i, n_steps, wout_ref, wp_ref, ps_ref, wout_s):
    a_width = A_HEADS * A_HEAD_DIM
    chunk_rows = wout_ref.shape[0]
    for step in range(n_steps):
        @pl.when(i == step)
        def _(step=step):
            r0 = step * chunk_rows
            if r0 < a_width:
                wout_s[r0:r0 + chunk_rows, :] = wout_ref[...].astype(jnp.bfloat16)
            else:
                for lr in range(0, chunk_rows, POOL_GROUP_DIM):
                    g = (r0 + lr - a_width) // POOL_GROUP_DIM
                    lanes = slice(g * POOL_GROUP_DIM, (g + 1) * POOL_GROUP_DIM)
                    folded = jnp.dot(wp_ref[g] * ps_ref[:, lanes], wout_ref[lr:lr + POOL_GROUP_DIM, :],
                                     precision=lax.Precision.HIGHEST,
                                     preferred_element_type=jnp.float32)
                    wout_s[r0 + lr:r0 + lr + POOL_GROUP_DIM, :] = folded.astype(jnp.bfloat16)


def _mixer_consts(ws_ref):
    row_head = lax.broadcasted_iota(jnp.int32, (V7X_MXU_DIM, V7X_MXU_DIM), 0) // A_HEAD_DIM
    col_head = lax.broadcasted_iota(jnp.int32, (V7X_MXU_DIM, V7X_MXU_DIM), 1) // A_HEAD_DIM
    seg = (row_head == col_head).astype(jnp.bfloat16)
    causal = (lax.broadcasted_iota(jnp.int32, (CHUNK, CHUNK), 1)
              <= lax.broadcasted_iota(jnp.int32, (CHUNK, CHUNK), 0))
    wcat = jnp.concatenate([jnp.where(causal, ws_ref[hh], 0.0).astype(jnp.bfloat16)
                            for hh in range(A_HEADS)], axis=1)
    lane_head = lax.broadcasted_iota(jnp.int32, (CHUNK, V7X_MXU_DIM), 1) // A_HEAD_DIM
    head_masks = [(lane_head == hh).astype(jnp.bfloat16) for hh in range(V7X_MXU_DIM // A_HEAD_DIM)]
    return seg, wcat, head_masks


def _mixer_kernel(x_ref, gm_ref, win_ref, vg_ref, ws_ref, bias_ref, wp_ref, ps_ref, wout_ref,
                  o_ref, win_s, wout_s, zprev_ref, *, tiles_per_seq):
    i = pl.program_id(0)

    @pl.when(i < M_PREP_STEPS)
    def _():
        _store_bf16_chunk(i, win_ref, win_s)

    _prep_wout(i, M_PREP_STEPS, wout_ref, wp_ref, ps_ref, wout_s)

    @pl.when(i >= M_PREP_STEPS)
    def _():
        tm = x_ref.shape[0]
        j = (i - M_PREP_STEPS) % tiles_per_seq
        consts = _mixer_consts(ws_ref)

        @pl.when(j == 0)
        def _():
            zprev_ref[...] = jnp.zeros_like(zprev_ref)

        halo = zprev_ref[...]
        for r in range(0, tm, MIX_ROWS):
            x2, halo = _mixer_rows(x_ref[r:r + MIX_ROWS, :], halo, j * tm + (r + 1), consts,
                                   gm_ref, win_s, vg_ref, bias_ref, wout_s)
            o_ref[r:r + MIX_ROWS, :] = x2
        zprev_ref[...] = halo


def _mixer_call(x3d, gm, w_in, v_gain, w_s, bias, w_pool, p_scale, w_out):
    b, s, d = x3d.shape
    d_in, d_mix = w_in.shape[1], w_out.shape[0]
    a_width = A_HEADS * A_HEAD_DIM
    b_width = len(POOL_WINDOWS) * POOL_GROUP_DIM
    assert s % MIX_TOKENS == 0 and MIX_TOKENS % MIX_ROWS == 0 and MIX_ROWS % CHUNK == 0
    assert d_in == 2 * a_width + b_width and d_mix == a_width + b_width
    assert d % (M_PREP_STEPS * BF16_SUBLANES) == 0 and d_mix % (M_PREP_STEPS * BF16_SUBLANES) == 0
    assert (d_mix // M_PREP_STEPS) % POOL_GROUP_DIM == 0 and a_width % (d_mix // M_PREP_STEPS) == 0
    tiles_per_seq = s // MIX_TOKENS
    full = lambda i: (0, 0)
    chunk = lambda i: (jnp.minimum(i, M_PREP_STEPS - 1), 0)

    def tile(i):
        t = jnp.maximum(i - M_PREP_STEPS, 0)
        return (t // tiles_per_seq, t % tiles_per_seq, 0)

    return pl.pallas_call(
        functools.partial(_mixer_kernel, tiles_per_seq=tiles_per_seq),
        grid=(M_PREP_STEPS + b * tiles_per_seq,),
        in_specs=[
            pl.BlockSpec((None, MIX_TOKENS, d), tile),
            pl.BlockSpec((1, d), full),
            pl.BlockSpec((d // M_PREP_STEPS, d_in), chunk),
            pl.BlockSpec((1, a_width), full),
            pl.BlockSpec(w_s.shape, lambda i: (0, 0, 0)),
            pl.BlockSpec(bias.shape, full),
            pl.BlockSpec(w_pool.shape, lambda i: (0, 0, 0)),
            pl.BlockSpec((1, b_width), full),
            pl.BlockSpec((d_mix // M_PREP_STEPS, d), chunk),
        ],
        out_specs=pl.BlockSpec((None, MIX_TOKENS, d), tile),
        out_shape=jax.ShapeDtypeStruct((b, s, d), jnp.float32),
        scratch_shapes=[pltpu.VMEM((d, d_in), jnp.bfloat16),
                        pltpu.VMEM((d_mix, d), jnp.bfloat16),
                        pltpu.VMEM((HALO, b_width), jnp.float32)],
        compiler_params=pltpu.CompilerParams(
            dimension_semantics=("arbitrary",),
            vmem_limit_bytes=V7X_VMEM_LIMIT_BYTES),
        name="mixer",
    )(x3d, gm, w_in, v_gain, w_s, bias, w_pool, p_scale, w_out)


def _ffn_kernel(x_ref, g_ref, wga_ref, wgb_ref, wua_ref, wub_ref, wda_ref, wdb_ref, fg_ref, o_ref,
                wg_s, wu_s, wd_s, h_s, acc_s, *, final_norm):
    i = pl.program_id(0)
    n_chunks = wg_s.shape[0]

    @pl.when(i == 0)
    def _():
        h_s[...] = _rmsnorm(x_ref[...], g_ref[...]).astype(jnp.bfloat16)
        acc_s[...] = jnp.zeros_like(acc_s)

    @pl.when(i < n_chunks)
    def _():
        wg = jnp.concatenate([wga_ref[...], wgb_ref[...]], axis=0).astype(jnp.bfloat16)
        wu = jnp.concatenate([wua_ref[...], wub_ref[...]], axis=0).astype(jnp.bfloat16)
        wd = (0.5 * jnp.concatenate([wda_ref[...], wdb_ref[...]], axis=0)).astype(jnp.bfloat16)
        wg_s[i] = wg
        wu_s[i] = wu
        wd_s[pl.ds(pl.multiple_of(i * FFN_COLS, FFN_COLS), FFN_COLS), :] = wd
        h = h_s[...]
        a = (_silu(_dot(h, wg)) * _dot(h, wu)).astype(jnp.bfloat16)
        acc_s[...] += _dot(a, wd)

    @pl.when(i == n_chunks - 1)
    def _():
        y = x_ref[...] + acc_s[...]
        if final_norm:
            y = _rmsnorm(y, fg_ref[...])
        o_ref[...] = y

    @pl.when(i >= n_chunks)
    def _():
        r = 0
        for rows in FFN_ROW_BLOCKS:
            y = _ffn_rows(x_ref[r:r + rows, :], g_ref, wg_s, wu_s, wd_s)
            if final_norm:
                y = _rmsnorm(y, fg_ref[...])
            o_ref[r:r + rows, :] = y
            r += rows


def _ffn_call(x2d, g, wg, wu, wd, fg, *, final_norm):
    n, d = x2d.shape
    d_ff = wg.shape[1]
    assert n % FFN_TOKENS == 0 and sum(FFN_ROW_BLOCKS) == FFN_TOKENS and d_ff % FFN_COLS == 0
    n_chunks = d_ff // FFN_COLS
    full = lambda i: (0, 0)
    tile = lambda i: (jnp.maximum(i - (n_chunks - 1), 0), 0)
    chunk = lambda i: jnp.minimum(i, n_chunks - 1)
    col_top, col_bot = (lambda i: (0, chunk(i))), (lambda i: (1, chunk(i)))
    row_top, row_bot = (lambda i: (2 * chunk(i), 0)), (lambda i: (2 * chunk(i) + 1, 0))
    return pl.pallas_call(
        functools.partial(_ffn_kernel, final_norm=final_norm),
        grid=(n_chunks - 1 + n // FFN_TOKENS,),
        in_specs=[
            pl.BlockSpec((FFN_TOKENS, d), tile),
            pl.BlockSpec((1, d), full),
            pl.BlockSpec((d // 2, FFN_COLS), col_top),
            pl.BlockSpec((d // 2, FFN_COLS), col_bot),
            pl.BlockSpec((d // 2, FFN_COLS), col_top),
            pl.BlockSpec((d // 2, FFN_COLS), col_bot),
            pl.BlockSpec((FFN_COLS // 2, d), row_top),
            pl.BlockSpec((FFN_COLS // 2, d), row_bot),
            pl.BlockSpec((1, d), full),
        ],
        out_specs=pl.BlockSpec((FFN_TOKENS, d), tile),
        out_shape=jax.ShapeDtypeStruct((n, d), jnp.float32),
        scratch_shapes=[pltpu.VMEM((n_chunks, d, FFN_COLS), jnp.bfloat16),
                        pltpu.VMEM((n_chunks, d, FFN_COLS), jnp.bfloat16),
                        pltpu.VMEM((d_ff, d), jnp.bfloat16),
                        pltpu.VMEM((FFN_TOKENS, d), jnp.bfloat16),
                        pltpu.VMEM((FFN_TOKENS, d), jnp.float32)],
        compiler_params=pltpu.CompilerParams(
            dimension_semantics=("arbitrary",),
            vmem_limit_bytes=V7X_VMEM_LIMIT_BYTES),
        name="ffn_final" if final_norm else "ffn",
    )(x2d, g, wg, wg, wu, wu, wd, wd, fg)


def kernel(x, ffn1_norm, ffn1_w_gate, ffn1_w_up, ffn1_w_down, mix_norm, w_in, gmlp_v_norm,
           gmlp_w_s, gmlp_b_s, pool_w, pool_scale, w_out, ffn2_norm, ffn2_w_gate, ffn2_w_up,
           ffn2_w_down, final_norm):
    bsz, seq, d = x.shape
    depth = ffn1_norm.shape[0]
    fg = final_norm.reshape(1, d)
    for l in range(depth):
        last = l == depth - 1
        x2d = _ffn_call(x.reshape(bsz * seq, d), ffn1_norm[l].reshape(1, d), ffn1_w_gate[l],
                        ffn1_w_up[l], ffn1_w_down[l], fg, final_norm=False)
        bias = jnp.repeat(gmlp_b_s[l].T, A_HEAD_DIM, axis=1)
        x3d = _mixer_call(x2d.reshape(bsz, seq, d), mix_norm[l].reshape(1, d), w_in[l],
                          gmlp_v_norm[l].reshape(1, -1), gmlp_w_s[l], bias, pool_w[l],
                          pool_scale[l].reshape(1, -1), w_out[l])
        x = _ffn_call(x3d.reshape(bsz * seq, d), ffn2_norm[l].reshape(1, d), ffn2_w_gate[l],
                      ffn2_w_up[l], ffn2_w_down[l], fg, final_norm=last).reshape(bsz, seq, d)
    return x
```

```python
import functools

import jax
import jax.numpy as jnp
from jax import lax
from jax.experimental import pallas as pl
from jax.experimental.pallas import tpu as pltpu

EPS = 1e-6
CHUNK = 128
A_HEADS = 8
A_HEAD_DIM = 64
POOL_WINDOWS = (2, 4, 8, 16)
POOL_GROUP_DIM = 128
HALO = 16
assert all(w & (w - 1) == 0 and w <= HALO for w in POOL_WINDOWS)

V7X_MXU_DIM = 256
V7X_VMEM_LIMIT_BYTES = 56 * 1024 * 1024
BF16_SUBLANES = 16

FFN_TOKENS = 1024
MIX_TOKENS = 1024
FFN_ROW_BLOCKS = (256, 256, 256, 256)
MIX_ROWS = 512
FFN_COLS = 256
M_PREP_STEPS = 2


def _rmsnorm(x, g):
    r = lax.rsqrt(jnp.mean(x * x, axis=-1, keepdims=True) + EPS)
    return (x * r) * g


def _silu(x):
    return x * (0.5 * jnp.tanh(0.5 * x) + 0.5)


def _gelu_tanh(x):
    c = 0.7978845608028654
    return 0.5 * x * (1.0 + jnp.tanh(c * (x + 0.044715 * (x * x * x))))


def _dot(a, b):
    return jnp.dot(a, b, preferred_element_type=jnp.float32)


def _store_bf16_chunk(i, src_ref, dst_ref):
    rows = src_ref.shape[0]
    r0 = pl.multiple_of(i * rows, rows)
    dst_ref[pl.ds(r0, rows), :] = src_ref[...].astype(jnp.bfloat16)


def _ffn_rows(x, g_ref, wg_s, wu_s, wd_s):
    h = _rmsnorm(x, g_ref[...]).astype(jnp.bfloat16)
    acts = []
    for c in range(wg_s.shape[0]):
        gate = _dot(h, wg_s[c])
        up = _dot(h, wu_s[c])
        acts.append((_silu(gate) * up).astype(jnp.bfloat16))
    a = jnp.concatenate(acts, axis=1)
    return x + _dot(a, wd_s[...])


def _mixer_rows(x, halo, first_pos, consts, g_ref, win_s, vg_ref, wout_s):
    seg, wcat, head_masks, bias = consts
    rows = x.shape[0]
    a_width = A_HEADS * A_HEAD_DIM
    heads_per_group = len(head_masks)

    h = _rmsnorm(x, g_ref[...]).astype(jnp.bfloat16)
    p = _dot(h, win_s[...])
    u = _gelu_tanh(p[:, :a_width])
    v = _gelu_tanh(p[:, a_width:2 * a_width])
    z = p[:, 2 * a_width:]

    vv = (v * v).astype(jnp.bfloat16)
    ms = jnp.concatenate(
        [_dot(vv[:, gs:gs + V7X_MXU_DIM], seg) for gs in range(0, a_width, V7X_MXU_DIM)],
        axis=1) * (1.0 / A_HEAD_DIM)
    vn = ((v * lax.rsqrt(ms + EPS)) * vg_ref[...]).astype(jnp.bfloat16)

    ya_chunks = []
    for c in range(0, rows, CHUNK):
        groups = []
        for gi in range(a_width // V7X_MXU_DIM):
            vg = vn[c:c + CHUNK, gi * V7X_MXU_DIM:(gi + 1) * V7X_MXU_DIM]
            stack = jnp.concatenate([vg * m for m in head_masks], axis=0)
            k0 = gi * heads_per_group * CHUNK
            groups.append(_dot(wcat[:, k0:k0 + heads_per_group * CHUNK], stack))
        mixed = jnp.concatenate(groups, axis=1) + bias
        ya_chunks.append(u[c:c + CHUNK, :] * mixed)
    y_a = jnp.concatenate(ya_chunks, axis=0)

    zext = jnp.concatenate([halo, z], axis=0)
    head_pos = first_pos + lax.broadcasted_iota(jnp.int32, (HALO, POOL_GROUP_DIM), 0)
    d_groups = []
    for gi, win in enumerate(POOL_WINDOWS):
        lanes = slice(gi * POOL_GROUP_DIM, (gi + 1) * POOL_GROUP_DIM)
        s = zext[:, lanes]
        step = 1
        while step < win:
            s = s + pltpu.roll(s, step, 0)
            step *= 2
        head_count = jnp.minimum(head_pos, win).astype(jnp.float32)
        mean = jnp.concatenate([s[HALO:2 * HALO, :] / head_count,
                                s[2 * HALO:, :] * (1.0 / win)], axis=0)
        d_groups.append((mean - z[:, lanes]).astype(jnp.bfloat16))

    y = jnp.concatenate([y_a.astype(jnp.bfloat16)] + d_groups, axis=1)
    return x + _dot(y, wout_s[...]), z[rows - HALO:, :]


def _prep_wout(i, n_steps, wout_ref, wp_ref, ps_ref, wout_s):
    a_width = A_HEADS * A_HEAD_DIM
    chunk_rows = wout_ref.shape[0]
    for step in range(n_steps):
        @pl.when(i == step)
        def _(step=step):
            r0 = step * chunk_rows
            if r0 < a_width:
                wout_s[r0:r0 + chunk_rows, :] = wout_ref[...].astype(jnp.bfloat16)
            else:
                for lr in range(0, chunk_rows, POOL_GROUP_DIM):
                    g = (r0 + lr - a_width) // POOL_GROUP_DIM
                    lanes = slice(g * POOL_GROUP_DIM, (g + 1) * POOL_GROUP_DIM)
                    folded = jnp.dot(wp_ref[g] * ps_ref[:, lanes], wout_ref[lr:lr + POOL_GROUP_DIM, :],
                                     precision=lax.Precision.HIGHEST,
                                     preferred_element_type=jnp.float32)
                    wout_s[r0 + lr:r0 + lr + POOL_GROUP_DIM, :] = folded.astype(jnp.bfloat16)


def _mixer_consts(ws_ref):
    row_head = lax.broadcasted_iota(jnp.int32, (V7X_MXU_DIM, V7X_MXU_DIM), 0) // A_HEAD_DIM
    col_head = lax.broadcasted_iota(jnp.int32, (V7X_MXU_DIM, V7X_MXU_DIM), 1) // A_HEAD_DIM
    seg = (row_head == col_head).astype(jnp.bfloat16)
    causal = (lax.broadcasted_iota(jnp.int32, (CHUNK, CHUNK), 1)
              <= lax.broadcasted_iota(jnp.int32, (CHUNK, CHUNK), 0))
    wcat = jnp.concatenate([jnp.where(causal, ws_ref[hh], 0.0).astype(jnp.bfloat16)
                            for hh in range(A_HEADS)], axis=1)
    lane_head = lax.broadcasted_iota(jnp.int32, (CHUNK, V7X_MXU_DIM), 1) // A_HEAD_DIM
    head_masks = [(lane_head == hh).astype(jnp.bfloat16) for hh in range(V7X_MXU_DIM // A_HEAD_DIM)]
    return seg, wcat, head_masks


def _position_bias(bs_ref):
    a_width = A_HEADS * A_HEAD_DIM
    bt = bs_ref[...].T
    bias_head = lax.broadcasted_iota(jnp.int32, (CHUNK, a_width), 1) // A_HEAD_DIM
    bias = jnp.zeros((CHUNK, a_width), jnp.float32)
    for hh in range(A_HEADS):
        bias = jnp.where(bias_head == hh, bt[:, hh:hh + 1], bias)
    return bias


def _mixer_kernel(x_ref, gm_ref, win_ref, vg_ref, ws_ref, bs_ref, wp_ref, ps_ref, wout_ref,
                  o_ref, win_s, wout_s, zprev_ref, bias_s, *, tiles_per_seq):
    i = pl.program_id(0)

    @pl.when(i == 0)
    def _():
        bias_s[...] = _position_bias(bs_ref)

    @pl.when(i < M_PREP_STEPS)
    def _():
        _store_bf16_chunk(i, win_ref, win_s)

    _prep_wout(i, M_PREP_STEPS, wout_ref, wp_ref, ps_ref, wout_s)

    @pl.when(i >= M_PREP_STEPS)
    def _():
        tm = x_ref.shape[0]
        j = (i - M_PREP_STEPS) % tiles_per_seq
        consts = _mixer_consts(ws_ref) + (bias_s[...],)

        @pl.when(j == 0)
        def _():
            zprev_ref[...] = jnp.zeros_like(zprev_ref)

        halo = zprev_ref[...]
        for r in range(0, tm, MIX_ROWS):
            x2, halo = _mixer_rows(x_ref[r:r + MIX_ROWS, :], halo, j * tm + (r + 1), consts,
                                   gm_ref, win_s, vg_ref, wout_s)
            o_ref[r:r + MIX_ROWS, :] = x2
        zprev_ref[...] = halo


def _mixer_call(x3d, gm, w_in, v_gain, w_s, b_s, w_pool, p_scale, w_out):
    b, s, d = x3d.shape
    d_in, d_mix = w_in.shape[1], w_out.shape[0]
    a_width = A_HEADS * A_HEAD_DIM
    b_width = len(POOL_WINDOWS) * POOL_GROUP_DIM
    assert s % MIX_TOKENS == 0 and MIX_TOKENS % MIX_ROWS == 0 and MIX_ROWS % CHUNK == 0
    assert d_in == 2 * a_width + b_width and d_mix == a_width + b_width
    assert d % (M_PREP_STEPS * BF16_SUBLANES) == 0 and d_mix % (M_PREP_STEPS * BF16_SUBLANES) == 0
    assert (d_mix // M_PREP_STEPS) % POOL_GROUP_DIM == 0 and a_width % (d_mix // M_PREP_STEPS) == 0
    tiles_per_seq = s // MIX_TOKENS
    full = lambda i: (0, 0)
    chunk = lambda i: (jnp.minimum(i, M_PREP_STEPS - 1), 0)

    def tile(i):
        t = jnp.maximum(i - M_PREP_STEPS, 0)
        return (t // tiles_per_seq, t % tiles_per_seq, 0)

    return pl.pallas_call(
        functools.partial(_mixer_kernel, tiles_per_seq=tiles_per_seq),
        grid=(M_PREP_STEPS + b * tiles_per_seq,),
        in_specs=[
            pl.BlockSpec((None, MIX_TOKENS, d), tile),
            pl.BlockSpec((1, d), full),
            pl.BlockSpec((d // M_PREP_STEPS, d_in), chunk),
            pl.BlockSpec((1, a_width), full),
            pl.BlockSpec(w_s.shape, lambda i: (0, 0, 0)),
            pl.BlockSpec(b_s.shape, full),
            pl.BlockSpec(w_pool.shape, lambda i: (0, 0, 0)),
            pl.BlockSpec((1, b_width), full),
            pl.BlockSpec((d_mix // M_PREP_STEPS, d), chunk),
        ],
        out_specs=pl.BlockSpec((None, MIX_TOKENS, d), tile),
        out_shape=jax.ShapeDtypeStruct((b, s, d), jnp.float32),
        scratch_shapes=[pltpu.VMEM((d, d_in), jnp.bfloat16),
                        pltpu.VMEM((d_mix, d), jnp.bfloat16),
                        pltpu.VMEM((HALO, b_width), jnp.float32),
                        pltpu.VMEM((CHUNK, a_width), jnp.float32)],
        compiler_params=pltpu.CompilerParams(
            dimension_semantics=("arbitrary",),
            vmem_limit_bytes=V7X_VMEM_LIMIT_BYTES),
        name="mixer",
    )(x3d, gm, w_in, v_gain, w_s, b_s, w_pool, p_scale, w_out)


def _ffn_kernel(x_ref, g_ref, wga_ref, wgb_ref, wua_ref, wub_ref, wda_ref, wdb_ref, fg_ref, o_ref,
                wg_s, wu_s, wd_s, h_s, acc_s, *, final_norm):
    i = pl.program_id(0)
    n_chunks = wg_s.shape[0]

    @pl.when(i == 0)
    def _():
        h_s[...] = _rmsnorm(x_ref[...], g_ref[...]).astype(jnp.bfloat16)
        acc_s[...] = jnp.zeros_like(acc_s)

    @pl.when(i < n_chunks)
    def _():
        wg = jnp.concatenate([wga_ref[...], wgb_ref[...]], axis=0).astype(jnp.bfloat16)
        wu = jnp.concatenate([wua_ref[...], wub_ref[...]], axis=0).astype(jnp.bfloat16)
        wd = (0.5 * jnp.concatenate([wda_ref[...], wdb_ref[...]], axis=0)).astype(jnp.bfloat16)
        wg_s[i] = wg
        wu_s[i] = wu
        wd_s[pl.ds(pl.multiple_of(i * FFN_COLS, FFN_COLS), FFN_COLS), :] = wd
        h = h_s[...]
        a = (_silu(_dot(h, wg)) * _dot(h, wu)).astype(jnp.bfloat16)
        acc_s[...] += _dot(a, wd)

    @pl.when(i == n_chunks - 1)
    def _():
        y = x_ref[...] + acc_s[...]
        if final_norm:
            y = _rmsnorm(y, fg_ref[...])
        o_ref[...] = y

    @pl.when(i >= n_chunks)
    def _():
        r = 0
        for rows in FFN_ROW_BLOCKS:
            y = _ffn_rows(x_ref[r:r + rows, :], g_ref, wg_s, wu_s, wd_s)
            if final_norm:
                y = _rmsnorm(y, fg_ref[...])
            o_ref[r:r + rows, :] = y
            r += rows


def _ffn_call(x2d, g, wg, wu, wd, fg, *, final_norm):
    n, d = x2d.shape
    d_ff = wg.shape[1]
    assert n % FFN_TOKENS == 0 and sum(FFN_ROW_BLOCKS) == FFN_TOKENS and d_ff % FFN_COLS == 0
    n_chunks = d_ff // FFN_COLS
    full = lambda i: (0, 0)
    tile = lambda i: (jnp.maximum(i - (n_chunks - 1), 0), 0)
    chunk = lambda i: jnp.minimum(i, n_chunks - 1)
    col_top, col_bot = (lambda i: (0, chunk(i))), (lambda i: (1, chunk(i)))
    row_top, row_bot = (lambda i: (2 * chunk(i), 0)), (lambda i: (2 * chunk(i) + 1, 0))
    return pl.pallas_call(
        functools.partial(_ffn_kernel, final_norm=final_norm),
        grid=(n_chunks - 1 + n // FFN_TOKENS,),
        in_specs=[
            pl.BlockSpec((FFN_TOKENS, d), tile),
            pl.BlockSpec((1, d), full),
            pl.BlockSpec((d // 2, FFN_COLS), col_top),
            pl.BlockSpec((d // 2, FFN_COLS), col_bot),
            pl.BlockSpec((d // 2, FFN_COLS), col_top),
            pl.BlockSpec((d // 2, FFN_COLS), col_bot),
            pl.BlockSpec((FFN_COLS // 2, d), row_top),
            pl.BlockSpec((FFN_COLS // 2, d), row_bot),
            pl.BlockSpec((1, d), full),
        ],
        out_specs=pl.BlockSpec((FFN_TOKENS, d), tile),
        out_shape=jax.ShapeDtypeStruct((n, d), jnp.float32),
        scratch_shapes=[pltpu.VMEM((n_chunks, d, FFN_COLS), jnp.bfloat16),
                        pltpu.VMEM((n_chunks, d, FFN_COLS), jnp.bfloat16),
                        pltpu.VMEM((d_ff, d), jnp.bfloat16),
                        pltpu.VMEM((FFN_TOKENS, d), jnp.bfloat16),
                        pltpu.VMEM((FFN_TOKENS, d), jnp.float32)],
        compiler_params=pltpu.CompilerParams(
            dimension_semantics=("arbitrary",),
            vmem_limit_bytes=V7X_VMEM_LIMIT_BYTES),
        name="ffn_final" if final_norm else "ffn",
    )(x2d, g, wg, wg, wu, wu, wd, wd, fg)


def kernel(x, ffn1_norm, ffn1_w_gate, ffn1_w_up, ffn1_w_down, mix_norm, w_in, gmlp_v_norm,
           gmlp_w_s, gmlp_b_s, pool_w, pool_scale, w_out, ffn2_norm, ffn2_w_gate, ffn2_w_up,
           ffn2_w_down, final_norm):
    bsz, seq, d = x.shape
    depth = ffn1_norm.shape[0]
    fg = final_norm.reshape(1, d)
    for l in range(depth):
        last = l == depth - 1
        x2d = _ffn_call(x.reshape(bsz * seq, d), ffn1_norm[l].reshape(1, d), ffn1_w_gate[l],
                        ffn1_w_up[l], ffn1_w_down[l], fg, final_norm=False)
        x3d = _mixer_call(x2d.reshape(bsz, seq, d), mix_norm[l].reshape(1, d), w_in[l],
                          gmlp_v_norm[l].reshape(1, -1), gmlp_w_s[l], gmlp_b_s[l], pool_w[l],
                          pool_scale[l].reshape(1, -1), w_out[l])
        x = _ffn_call(x3d.reshape(bsz * seq, d), ffn2_norm[l].reshape(1, d), ffn2_w_gate[l],
                      ffn2_w_up[l], ffn2_w_down[l], fg, final_norm=last).reshape(bsz, seq, d)
    return x
```

```python
import functools

import jax
import jax.numpy as jnp
from jax import lax
from jax.experimental import pallas as pl
from jax.experimental.pallas import tpu as pltpu

EPS = 1e-6
CHUNK = 128
A_HEADS = 8
A_HEAD_DIM = 64
POOL_WINDOWS = (2, 4, 8, 16)
POOL_GROUP_DIM = 128
HALO = 16
assert all(w & (w - 1) == 0 and w <= HALO for w in POOL_WINDOWS)

V7X_MXU_DIM = 256
V7X_VMEM_LIMIT_BYTES = 56 * 1024 * 1024
BF16_SUBLANES = 16

FFN_TOKENS = 1024
MIX_TOKENS = 2048
FFN_ROW_BLOCKS = (256, 256, 256, 256)
MIX_ROWS = 512
FFN_COLS = 256
M_PREP_STEPS = 2


def _rmsnorm(x, g):
    r = lax.rsqrt(jnp.mean(x * x, axis=-1, keepdims=True) + EPS)
    return (x * r) * g


def _silu(x):
    return x * (0.5 * jnp.tanh(0.5 * x) + 0.5)


def _gelu_tanh(x):
    c = 0.7978845608028654
    return 0.5 * x * (1.0 + jnp.tanh(c * (x + 0.044715 * (x * x * x))))


def _dot(a, b):
    return jnp.dot(a, b, preferred_element_type=jnp.float32)


def _store_bf16_chunk(i, src_ref, dst_ref):
    rows = src_ref.shape[0]
    r0 = pl.multiple_of(i * rows, rows)
    dst_ref[pl.ds(r0, rows), :] = src_ref[...].astype(jnp.bfloat16)


def _ffn_rows(x, g_ref, wg_s, wu_s, wd_s):
    h = _rmsnorm(x, g_ref[...]).astype(jnp.bfloat16)
    acts = []
    for c in range(wg_s.shape[0]):
        gate = _dot(h, wg_s[c])
        up = _dot(h, wu_s[c])
        acts.append((_silu(gate) * up).astype(jnp.bfloat16))
    a = jnp.concatenate(acts, axis=1)
    return x + _dot(a, wd_s[...])


def _mixer_rows(x, halo, first_pos, consts, g_ref, win_s, vg_ref, wout_s):
    seg, wcat, head_masks, bias = consts
    rows = x.shape[0]
    a_width = A_HEADS * A_HEAD_DIM
    heads_per_group = len(head_masks)

    h = _rmsnorm(x, g_ref[...]).astype(jnp.bfloat16)
    p = _dot(h, win_s[...])
    u = _gelu_tanh(p[:, :a_width])
    v = _gelu_tanh(p[:, a_width:2 * a_width])
    z = p[:, 2 * a_width:]

    vv = (v * v).astype(jnp.bfloat16)
    ms = jnp.concatenate(
        [_dot(vv[:, gs:gs + V7X_MXU_DIM], seg) for gs in range(0, a_width, V7X_MXU_DIM)],
        axis=1) * (1.0 / A_HEAD_DIM)
    vn = ((v * lax.rsqrt(ms + EPS)) * vg_ref[...]).astype(jnp.bfloat16)

    ya_chunks = []
    for c in range(0, rows, CHUNK):
        groups = []
        for gi in range(a_width // V7X_MXU_DIM):
            vg = vn[c:c + CHUNK, gi * V7X_MXU_DIM:(gi + 1) * V7X_MXU_DIM]
            stack = jnp.concatenate([vg * m for m in head_masks], axis=0)
            k0 = gi * heads_per_group * CHUNK
            groups.append(_dot(wcat[:, k0:k0 + heads_per_group * CHUNK], stack))
        mixed = jnp.concatenate(groups, axis=1) + bias
        ya_chunks.append(u[c:c + CHUNK, :] * mixed)
    y_a = jnp.concatenate(ya_chunks, axis=0)

    zext = jnp.concatenate([halo, z], axis=0)
    head_pos = first_pos + lax.broadcasted_iota(jnp.int32, (HALO, POOL_GROUP_DIM), 0)
    d_groups = []
    for gi, win in enumerate(POOL_WINDOWS):
        lanes = slice(gi * POOL_GROUP_DIM, (gi + 1) * POOL_GROUP_DIM)
        s = zext[:, lanes]
        step = 1
        while step < win:
            s = s + pltpu.roll(s, step, 0)
            step *= 2
        head_count = jnp.minimum(head_pos, win).astype(jnp.float32)
        mean = jnp.concatenate([s[HALO:2 * HALO, :] / head_count,
                                s[2 * HALO:, :] * (1.0 / win)], axis=0)
        d_groups.append((mean - z[:, lanes]).astype(jnp.bfloat16))

    y = jnp.concatenate([y_a.astype(jnp.bfloat16)] + d_groups, axis=1)
    return x + _dot(y, wout_s[...]), z[rows - HALO:, :]


def _prep_wout(i, n_steps, wout_ref, wp_ref, ps_ref, wout_s):
    a_width = A_HEADS * A_HEAD_DIM
    chunk_rows = wout_ref.shape[0]
    for step in range(n_steps):
        @pl.when(i == step)
        def _(step=step):
            r0 = step * chunk_rows
            if r0 < a_width:
                wout_s[r0:r0 + chunk_rows, :] = wout_ref[...].astype(jnp.bfloat16)
            else:
                for lr in range(0, chunk_rows, POOL_GROUP_DIM):
                    g = (r0 + lr - a_width) // POOL_GROUP_DIM
                    lanes = slice(g * POOL_GROUP_DIM, (g + 1) * POOL_GROUP_DIM)
                    folded = jnp.dot(wp_ref[g] * ps_ref[:, lanes], wout_ref[lr:lr + POOL_GROUP_DIM, :],
                                     precision=lax.Precision.HIGHEST,
                                     preferred_element_type=jnp.float32)
                    wout_s[r0 + lr:r0 + lr + POOL_GROUP_DIM, :] = folded.astype(jnp.bfloat16)


def _mixer_consts(ws_ref):
    row_head = lax.broadcasted_iota(jnp.int32, (V7X_MXU_DIM, V7X_MXU_DIM), 0) // A_HEAD_DIM
    col_head = lax.broadcasted_iota(jnp.int32, (V7X_MXU_DIM, V7X_MXU_DIM), 1) // A_HEAD_DIM
    seg = (row_head == col_head).astype(jnp.bfloat16)
    causal = (lax.broadcasted_iota(jnp.int32, (CHUNK, CHUNK), 1)
              <= lax.broadcasted_iota(jnp.int32, (CHUNK, CHUNK), 0))
    wcat = jnp.concatenate([jnp.where(causal, ws_ref[hh], 0.0).astype(jnp.bfloat16)
                            for hh in range(A_HEADS)], axis=1)
    lane_head = lax.broadcasted_iota(jnp.int32, (CHUNK, V7X_MXU_DIM), 1) // A_HEAD_DIM
    head_masks = [(lane_head == hh).astype(jnp.bfloat16) for hh in range(V7X_MXU_DIM // A_HEAD_DIM)]
    return seg, wcat, head_masks


def _position_bias(bs_ref):
    a_width = A_HEADS * A_HEAD_DIM
    bt = bs_ref[...].T
    bias_head = lax.broadcasted_iota(jnp.int32, (CHUNK, a_width), 1) // A_HEAD_DIM
    bias = jnp.zeros((CHUNK, a_width), jnp.float32)
    for hh in range(A_HEADS):
        bias = jnp.where(bias_head == hh, bt[:, hh:hh + 1], bias)
    return bias


def _mixer_kernel(x_ref, gm_ref, win_ref, vg_ref, ws_ref, bs_ref, wp_ref, ps_ref, wout_ref,
                  o_ref, win_s, wout_s, zprev_ref, bias_s, *, tiles_per_seq):
    i = pl.program_id(0)

    @pl.when(i == 0)
    def _():
        bias_s[...] = _position_bias(bs_ref)

    @pl.when(i < M_PREP_STEPS)
    def _():
        _store_bf16_chunk(i, win_ref, win_s)

    _prep_wout(i, M_PREP_STEPS, wout_ref, wp_ref, ps_ref, wout_s)

    @pl.when(i >= M_PREP_STEPS)
    def _():
        tm = x_ref.shape[0]
        j = (i - M_PREP_STEPS) % tiles_per_seq
        consts = _mixer_consts(ws_ref) + (bias_s[...],)

        @pl.when(j == 0)
        def _():
            zprev_ref[...] = jnp.zeros_like(zprev_ref)

        halo = zprev_ref[...]
        for r in range(0, tm, MIX_ROWS):
            x2, halo = _mixer_rows(x_ref[r:r + MIX_ROWS, :], halo, j * tm + (r + 1), consts,
                                   gm_ref, win_s, vg_ref, wout_s)
            o_ref[r:r + MIX_ROWS, :] = x2
        zprev_ref[...] = halo


def _mixer_call(x3d, gm, w_in, v_gain, w_s, b_s, w_pool, p_scale, w_out):
    b, s, d = x3d.shape
    d_in, d_mix = w_in.shape[1], w_out.shape[0]
    a_width = A_HEADS * A_HEAD_DIM
    b_width = len(POOL_WINDOWS) * POOL_GROUP_DIM
    assert s % MIX_TOKENS == 0 and MIX_TOKENS % MIX_ROWS == 0 and MIX_ROWS % CHUNK == 0
    assert d_in == 2 * a_width + b_width and d_mix == a_width + b_width
    assert d % (M_PREP_STEPS * BF16_SUBLANES) == 0 and d_mix % (M_PREP_STEPS * BF16_SUBLANES) == 0
    assert (d_mix // M_PREP_STEPS) % POOL_GROUP_DIM == 0 and a_width % (d_mix // M_PREP_STEPS) == 0
    tiles_per_seq = s // MIX_TOKENS
    full = lambda i: (0, 0)
    chunk = lambda i: (jnp.minimum(i, M_PREP_STEPS - 1), 0)

    def tile(i):
        t = jnp.maximum(i - M_PREP_STEPS, 0)
        return (t // tiles_per_seq, t % tiles_per_seq, 0)

    return pl.pallas_call(
        functools.partial(_mixer_kernel, tiles_per_seq=tiles_per_seq),
        grid=(M_PREP_STEPS + b * tiles_per_seq,),
        in_specs=[
            pl.BlockSpec((None, MIX_TOKENS, d), tile),
            pl.BlockSpec((1, d), full),
            pl.BlockSpec((d // M_PREP_STEPS, d_in), chunk),
            pl.BlockSpec((1, a_width), full),
            pl.BlockSpec(w_s.shape, lambda i: (0, 0, 0)),
            pl.BlockSpec(b_s.shape, full),
            pl.BlockSpec(w_pool.shape, lambda i: (0, 0, 0)),
            pl.BlockSpec((1, b_width), full),
            pl.BlockSpec((d_mix // M_PREP_STEPS, d), chunk),
        ],
        out_specs=pl.BlockSpec((None, MIX_TOKENS, d), tile),
        out_shape=jax.ShapeDtypeStruct((b, s, d), jnp.float32),
        scratch_shapes=[pltpu.VMEM((d, d_in), jnp.bfloat16),
                        pltpu.VMEM((d_mix, d), jnp.bfloat16),
                        pltpu.VMEM((HALO, b_width), jnp.float32),
                        pltpu.VMEM((CHUNK, a_width), jnp.float32)],
        compiler_params=pltpu.CompilerParams(
            dimension_semantics=("arbitrary",),
            vmem_limit_bytes=V7X_VMEM_LIMIT_BYTES),
        name="mixer",
    )(x3d, gm, w_in, v_gain, w_s, b_s, w_pool, p_scale, w_out)


def _ffn_kernel(x_ref, g_ref, wga_ref, wgb_ref, wua_ref, wub_ref, wda_ref, wdb_ref, fg_ref, o_ref,
                wg_s, wu_s, wd_s, h_s, acc_s, *, final_norm):
    i = pl.program_id(0)
    n_chunks = wg_s.shape[0]

    @pl.when(i == 0)
    def _():
        h_s[...] = _rmsnorm(x_ref[...], g_ref[...]).astype(jnp.bfloat16)
        acc_s[...] = jnp.zeros_like(acc_s)

    @pl.when(i < n_chunks)
    def _():
        wg = jnp.concatenate([wga_ref[...], wgb_ref[...]], axis=0).astype(jnp.bfloat16)
        wu = jnp.concatenate([wua_ref[...], wub_ref[...]], axis=0).astype(jnp.bfloat16)
        wd = (0.5 * jnp.concatenate([wda_ref[...], wdb_ref[...]], axis=0)).astype(jnp.bfloat16)
        wg_s[i] = wg
        wu_s[i] = wu
        wd_s[pl.ds(pl.multiple_of(i * FFN_COLS, FFN_COLS), FFN_COLS), :] = wd
        h = h_s[...]
        a = (_silu(_dot(h, wg)) * _dot(h, wu)).astype(jnp.bfloat16)
        acc_s[...] += _dot(a, wd)

    @pl.when(i == n_chunks - 1)
    def _():
        y = x_ref[...] + acc_s[...]
        if final_norm:
            y = _rmsnorm(y, fg_ref[...])
        o_ref[...] = y

    @pl.when(i >= n_chunks)
    def _():
        r = 0
        for rows in FFN_ROW_BLOCKS:
            y = _ffn_rows(x_ref[r:r + rows, :], g_ref, wg_s, wu_s, wd_s)
            if final_norm:
                y = _rmsnorm(y, fg_ref[...])
            o_ref[r:r + rows, :] = y
            r += rows


def _ffn_call(x2d, g, wg, wu, wd, fg, *, final_norm):
    n, d = x2d.shape
    d_ff = wg.shape[1]
    assert n % FFN_TOKENS == 0 and sum(FFN_ROW_BLOCKS) == FFN_TOKENS and d_ff % FFN_COLS == 0
    n_chunks = d_ff // FFN_COLS
    full = lambda i: (0, 0)
    tile = lambda i: (jnp.maximum(i - (n_chunks - 1), 0), 0)
    chunk = lambda i: jnp.minimum(i, n_chunks - 1)
    col_top, col_bot = (lambda i: (0, chunk(i))), (lambda i: (1, chunk(i)))
    row_top, row_bot = (lambda i: (2 * chunk(i), 0)), (lambda i: (2 * chunk(i) + 1, 0))
    return pl.pallas_call(
        functools.partial(_ffn_kernel, final_norm=final_norm),
        grid=(n_chunks - 1 + n // FFN_TOKENS,),
        in_specs=[
            pl.BlockSpec((FFN_TOKENS, d), tile),
            pl.BlockSpec((1, d), full),
            pl.BlockSpec((d // 2, FFN_COLS), col_top),
            pl.BlockSpec((d // 2, FFN_COLS), col_bot),
            pl.BlockSpec((d // 2, FFN_COLS), col_top),
            pl.BlockSpec((d // 2, FFN_COLS), col_bot),
            pl.BlockSpec((FFN_COLS // 2, d), row_top),
            pl.BlockSpec((FFN_COLS // 2, d), row_bot),
            pl.BlockSpec((1, d), full),
        ],
        out_specs=pl.BlockSpec((FFN_TOKENS, d), tile),
        out_shape=jax.ShapeDtypeStruct((n, d), jnp.float32),
        scratch_shapes=[pltpu.VMEM((n_chunks, d, FFN_COLS), jnp.bfloat16),
                        pltpu.VMEM((n_chunks, d, FFN_COLS), jnp.bfloat16),
                        pltpu.VMEM((d_ff, d), jnp.bfloat16),
                        pltpu.VMEM((FFN_TOKENS, d), jnp.bfloat16),
                        pltpu.VMEM((FFN_TOKENS, d), jnp.float32)],
        compiler_params=pltpu.CompilerParams(
            dimension_semantics=("arbitrary",),
            vmem_limit_bytes=V7X_VMEM_LIMIT_BYTES),
        name="ffn_final" if final_norm else "ffn",
    )(x2d, g, wg, wg, wu, wu, wd, wd, fg)


def kernel(x, ffn1_norm, ffn1_w_gate, ffn1_w_up, ffn1_w_down, mix_norm, w_in, gmlp_v_norm,
           gmlp_w_s, gmlp_b_s, pool_w, pool_scale, w_out, ffn2_norm, ffn2_w_gate, ffn2_w_up,
           ffn2_w_down, final_norm):
    bsz, seq, d = x.shape
    depth = ffn1_norm.shape[0]
    fg = final_norm.reshape(1, d)
    for l in range(depth):
        last = l == depth - 1
        x2d = _ffn_call(x.reshape(bsz * seq, d), ffn1_norm[l].reshape(1, d), ffn1_w_gate[l],
                        ffn1_w_up[l], ffn1_w_down[l], fg, final_norm=False)
        x3d = _mixer_call(x2d.reshape(bsz, seq, d), mix_norm[l].reshape(1, d), w_in[l],
                          gmlp_v_norm[l].reshape(1, -1), gmlp_w_s[l], gmlp_b_s[l], pool_w[l],
                          pool_scale[l].reshape(1, -1), w_out[l])
        x = _ffn_call(x3d.reshape(bsz * seq, d), ffn2_norm[l].reshape(1, d), ffn2_w_gate[l],
                      ffn2_w_up[l], ffn2_w_down[l], fg, final_norm=last).reshape(bsz, seq, d)
    return x
```

```python
import functools

import jax
import jax.numpy as jnp
from jax import lax
from jax.experimental import pallas as pl
from jax.experimental.pallas import tpu as pltpu

EPS = 1e-6
CHUNK = 128
A_HEADS = 8
A_HEAD_DIM = 64
POOL_WINDOWS = (2, 4, 8, 16)
POOL_GROUP_DIM = 128
HALO = 16
assert all(w & (w - 1) == 0 and w <= HALO for w in POOL_WINDOWS)

V7X_MXU_DIM = 256
V7X_VMEM_LIMIT_BYTES = 56 * 1024 * 1024
BF16_SUBLANES = 16

FFN_TOKENS = 1024
MIX_TOKENS = 2048
FFN_ROW_BLOCKS = (256, 256, 256, 256)
MIX_ROWS = 512
FFN_COLS = 256
M_PREP_STEPS = 2
W_RING = 4


def _rmsnorm(x, g):
    r = lax.rsqrt(jnp.mean(x * x, axis=-1, keepdims=True) + EPS)
    return (x * r) * g


def _silu(x):
    return x * (0.5 * jnp.tanh(0.5 * x) + 0.5)


def _gelu_tanh(x):
    c = 0.7978845608028654
    return 0.5 * x * (1.0 + jnp.tanh(c * (x + 0.044715 * (x * x * x))))


def _dot(a, b):
    return jnp.dot(a, b, preferred_element_type=jnp.float32)


def _store_bf16_chunk(i, src_ref, dst_ref):
    rows = src_ref.shape[0]
    r0 = pl.multiple_of(i * rows, rows)
    dst_ref[pl.ds(r0, rows), :] = src_ref[...].astype(jnp.bfloat16)


def _ffn_rows(x, g_ref, wg_s, wu_s, wd_s):
    h = _rmsnorm(x, g_ref[...]).astype(jnp.bfloat16)
    acts = []
    for c in range(wg_s.shape[0]):
        gate = _dot(h, wg_s[c])
        up = _dot(h, wu_s[c])
        acts.append((_silu(gate) * up).astype(jnp.bfloat16))
    a = jnp.concatenate(acts, axis=1)
    return x + _dot(a, wd_s[...])


def _mixer_rows(x, halo, first_pos, consts, g_ref, win_s, vg_ref, wout_s):
    seg, wcat, head_masks, bias = consts
    rows = x.shape[0]
    a_width = A_HEADS * A_HEAD_DIM
    heads_per_group = len(head_masks)

    h = _rmsnorm(x, g_ref[...]).astype(jnp.bfloat16)
    p = _dot(h, win_s[...])
    u = _gelu_tanh(p[:, :a_width])
    v = _gelu_tanh(p[:, a_width:2 * a_width])
    z = p[:, 2 * a_width:]

    vv = (v * v).astype(jnp.bfloat16)
    ms = jnp.concatenate(
        [_dot(vv[:, gs:gs + V7X_MXU_DIM], seg) for gs in range(0, a_width, V7X_MXU_DIM)],
        axis=1) * (1.0 / A_HEAD_DIM)
    vn = ((v * lax.rsqrt(ms + EPS)) * vg_ref[...]).astype(jnp.bfloat16)

    ya_chunks = []
    for c in range(0, rows, CHUNK):
        groups = []
        for gi in range(a_width // V7X_MXU_DIM):
            vg = vn[c:c + CHUNK, gi * V7X_MXU_DIM:(gi + 1) * V7X_MXU_DIM]
            stack = jnp.concatenate([vg * m for m in head_masks], axis=0)
            k0 = gi * heads_per_group * CHUNK
            groups.append(_dot(wcat[:, k0:k0 + heads_per_group * CHUNK], stack))
        mixed = jnp.concatenate(groups, axis=1) + bias
        ya_chunks.append(u[c:c + CHUNK, :] * mixed)
    y_a = jnp.concatenate(ya_chunks, axis=0)

    zext = jnp.concatenate([halo, z], axis=0)
    head_pos = first_pos + lax.broadcasted_iota(jnp.int32, (HALO, POOL_GROUP_DIM), 0)
    d_groups = []
    for gi, win in enumerate(POOL_WINDOWS):
        lanes = slice(gi * POOL_GROUP_DIM, (gi + 1) * POOL_GROUP_DIM)
        s = zext[:, lanes]
        step = 1
        while step < win:
            s = s + pltpu.roll(s, step, 0)
            step *= 2
        head_count = jnp.minimum(head_pos, win).astype(jnp.float32)
        mean = jnp.concatenate([s[HALO:2 * HALO, :] / head_count,
                                s[2 * HALO:, :] * (1.0 / win)], axis=0)
        d_groups.append((mean - z[:, lanes]).astype(jnp.bfloat16))

    y = jnp.concatenate([y_a.astype(jnp.bfloat16)] + d_groups, axis=1)
    return x + _dot(y, wout_s[...]), z[rows - HALO:, :]


def _prep_wout(i, n_steps, wout_ref, wp_ref, ps_ref, wout_s):
    a_width = A_HEADS * A_HEAD_DIM
    chunk_rows = wout_ref.shape[0]
    for step in range(n_steps):
        @pl.when(i == step)
        def _(step=step):
            r0 = step * chunk_rows
            if r0 < a_width:
                wout_s[r0:r0 + chunk_rows, :] = wout_ref[...].astype(jnp.bfloat16)
            else:
                for lr in range(0, chunk_rows, POOL_GROUP_DIM):
                    g = (r0 + lr - a_width) // POOL_GROUP_DIM
                    lanes = slice(g * POOL_GROUP_DIM, (g + 1) * POOL_GROUP_DIM)
                    folded = jnp.dot(wp_ref[g] * ps_ref[:, lanes], wout_ref[lr:lr + POOL_GROUP_DIM, :],
                                     precision=lax.Precision.HIGHEST,
                                     preferred_element_type=jnp.float32)
                    wout_s[r0 + lr:r0 + lr + POOL_GROUP_DIM, :] = folded.astype(jnp.bfloat16)


def _mixer_consts(ws_ref):
    row_head = lax.broadcasted_iota(jnp.int32, (V7X_MXU_DIM, V7X_MXU_DIM), 0) // A_HEAD_DIM
    col_head = lax.broadcasted_iota(jnp.int32, (V7X_MXU_DIM, V7X_MXU_DIM), 1) // A_HEAD_DIM
    seg = (row_head == col_head).astype(jnp.bfloat16)
    causal = (lax.broadcasted_iota(jnp.int32, (CHUNK, CHUNK), 1)
              <= lax.broadcasted_iota(jnp.int32, (CHUNK, CHUNK), 0))
    wcat = jnp.concatenate([jnp.where(causal, ws_ref[hh], 0.0).astype(jnp.bfloat16)
                            for hh in range(A_HEADS)], axis=1)
    lane_head = lax.broadcasted_iota(jnp.int32, (CHUNK, V7X_MXU_DIM), 1) // A_HEAD_DIM
    head_masks = [(lane_head == hh).astype(jnp.bfloat16) for hh in range(V7X_MXU_DIM // A_HEAD_DIM)]
    return seg, wcat, head_masks


def _position_bias(bs_ref):
    a_width = A_HEADS * A_HEAD_DIM
    bt = bs_ref[...].T
    bias_head = lax.broadcasted_iota(jnp.int32, (CHUNK, a_width), 1) // A_HEAD_DIM
    bias = jnp.zeros((CHUNK, a_width), jnp.float32)
    for hh in range(A_HEADS):
        bias = jnp.where(bias_head == hh, bt[:, hh:hh + 1], bias)
    return bias


def _mixer_kernel(x_ref, gm_ref, win_ref, vg_ref, ws_ref, bs_ref, wp_ref, ps_ref, wout_ref,
                  o_ref, win_s, wout_s, zprev_ref, bias_s, *, tiles_per_seq):
    i = pl.program_id(0)

    @pl.when(i == 0)
    def _():
        bias_s[...] = _position_bias(bs_ref)

    @pl.when(i < M_PREP_STEPS)
    def _():
        _store_bf16_chunk(i, win_ref, win_s)

    _prep_wout(i, M_PREP_STEPS, wout_ref, wp_ref, ps_ref, wout_s)

    @pl.when(i >= M_PREP_STEPS)
    def _():
        tm = x_ref.shape[0]
        j = (i - M_PREP_STEPS) % tiles_per_seq
        consts = _mixer_consts(ws_ref) + (bias_s[...],)

        @pl.when(j == 0)
        def _():
            zprev_ref[...] = jnp.zeros_like(zprev_ref)

        halo = zprev_ref[...]
        for r in range(0, tm, MIX_ROWS):
            x2, halo = _mixer_rows(x_ref[r:r + MIX_ROWS, :], halo, j * tm + (r + 1), consts,
                                   gm_ref, win_s, vg_ref, wout_s)
            o_ref[r:r + MIX_ROWS, :] = x2
        zprev_ref[...] = halo


def _mixer_call(x3d, gm, w_in, v_gain, w_s, b_s, w_pool, p_scale, w_out):
    b, s, d = x3d.shape
    d_in, d_mix = w_in.shape[1], w_out.shape[0]
    a_width = A_HEADS * A_HEAD_DIM
    b_width = len(POOL_WINDOWS) * POOL_GROUP_DIM
    assert s % MIX_TOKENS == 0 and MIX_TOKENS % MIX_ROWS == 0 and MIX_ROWS % CHUNK == 0
    assert d_in == 2 * a_width + b_width and d_mix == a_width + b_width
    assert d % (M_PREP_STEPS * BF16_SUBLANES) == 0 and d_mix % (M_PREP_STEPS * BF16_SUBLANES) == 0
    assert (d_mix // M_PREP_STEPS) % POOL_GROUP_DIM == 0 and a_width % (d_mix // M_PREP_STEPS) == 0
    tiles_per_seq = s // MIX_TOKENS
    full = lambda i: (0, 0)
    chunk = lambda i: (jnp.minimum(i, M_PREP_STEPS - 1), 0)

    def tile(i):
        t = jnp.maximum(i - M_PREP_STEPS, 0)
        return (t // tiles_per_seq, t % tiles_per_seq, 0)

    return pl.pallas_call(
        functools.partial(_mixer_kernel, tiles_per_seq=tiles_per_seq),
        grid=(M_PREP_STEPS + b * tiles_per_seq,),
        in_specs=[
            pl.BlockSpec((None, MIX_TOKENS, d), tile),
            pl.BlockSpec((1, d), full),
            pl.BlockSpec((d // M_PREP_STEPS, d_in), chunk),
            pl.BlockSpec((1, a_width), full),
            pl.BlockSpec(w_s.shape, lambda i: (0, 0, 0)),
            pl.BlockSpec(b_s.shape, full),
            pl.BlockSpec(w_pool.shape, lambda i: (0, 0, 0)),
            pl.BlockSpec((1, b_width), full),
            pl.BlockSpec((d_mix // M_PREP_STEPS, d), chunk),
        ],
        out_specs=pl.BlockSpec((None, MIX_TOKENS, d), tile),
        out_shape=jax.ShapeDtypeStruct((b, s, d), jnp.float32),
        scratch_shapes=[pltpu.VMEM((d, d_in), jnp.bfloat16),
                        pltpu.VMEM((d_mix, d), jnp.bfloat16),
                        pltpu.VMEM((HALO, b_width), jnp.float32),
                        pltpu.VMEM((CHUNK, a_width), jnp.float32)],
        compiler_params=pltpu.CompilerParams(
            dimension_semantics=("arbitrary",),
            vmem_limit_bytes=V7X_VMEM_LIMIT_BYTES),
        name="mixer",
    )(x3d, gm, w_in, v_gain, w_s, b_s, w_pool, p_scale, w_out)


def _ffn_kernel(x_ref, g_ref, wg_hbm, wu_hbm, wd_hbm, fg_ref, o_ref,
                wg_s, wu_s, wd_s, h_s, acc_s, wg_buf, wu_buf, wd_buf, sem, *, final_norm):
    i = pl.program_id(0)
    n_chunks = wg_s.shape[0]

    def chunk_copies(c, slot):
        cols = pl.ds(pl.multiple_of(c * FFN_COLS, FFN_COLS), FFN_COLS)
        return (pltpu.make_async_copy(wg_hbm.at[:, cols], wg_buf.at[slot], sem.at[0, slot]),
                pltpu.make_async_copy(wu_hbm.at[:, cols], wu_buf.at[slot], sem.at[1, slot]),
                pltpu.make_async_copy(wd_hbm.at[cols, :], wd_buf.at[slot], sem.at[2, slot]))

    @pl.when(i == 0)
    def _():
        for c in range(W_RING):
            for cp in chunk_copies(c, c):
                cp.start()
        h_s[...] = _rmsnorm(x_ref[...], g_ref[...]).astype(jnp.bfloat16)
        acc_s[...] = jnp.zeros_like(acc_s)

    @pl.when(i < n_chunks)
    def _():
        slot = i % W_RING
        for cp in chunk_copies(i, slot):
            cp.wait()
        wg = wg_buf[slot].astype(jnp.bfloat16)
        wu = wu_buf[slot].astype(jnp.bfloat16)
        wd = (0.5 * wd_buf[slot]).astype(jnp.bfloat16)

        @pl.when(i + W_RING < n_chunks)
        def _():
            for cp in chunk_copies(i + W_RING, slot):
                cp.start()

        wg_s[i] = wg
        wu_s[i] = wu
        wd_s[pl.ds(pl.multiple_of(i * FFN_COLS, FFN_COLS), FFN_COLS), :] = wd
        h = h_s[...]
        a = (_silu(_dot(h, wg)) * _dot(h, wu)).astype(jnp.bfloat16)
        acc_s[...] += _dot(a, wd)

    @pl.when(i == n_chunks - 1)
    def _():
        y = x_ref[...] + acc_s[...]
        if final_norm:
            y = _rmsnorm(y, fg_ref[...])
        o_ref[...] = y

    @pl.when(i >= n_chunks)
    def _():
        r = 0
        for rows in FFN_ROW_BLOCKS:
            y = _ffn_rows(x_ref[r:r + rows, :], g_ref, wg_s, wu_s, wd_s)
            if final_norm:
                y = _rmsnorm(y, fg_ref[...])
            o_ref[r:r + rows, :] = y
            r += rows


def _ffn_call(x2d, g, wg, wu, wd, fg, *, final_norm):
    n, d = x2d.shape
    d_ff = wg.shape[1]
    assert n % FFN_TOKENS == 0 and sum(FFN_ROW_BLOCKS) == FFN_TOKENS and d_ff % FFN_COLS == 0
    n_chunks = d_ff // FFN_COLS
    assert n_chunks >= W_RING
    full = lambda i: (0, 0)
    tile = lambda i: (jnp.maximum(i - (n_chunks - 1), 0), 0)
    in_hbm = pl.BlockSpec(memory_space=pl.ANY)
    return pl.pallas_call(
        functools.partial(_ffn_kernel, final_norm=final_norm),
        grid=(n_chunks - 1 + n // FFN_TOKENS,),
        in_specs=[
            pl.BlockSpec((FFN_TOKENS, d), tile),
            pl.BlockSpec((1, d), full),
            in_hbm, in_hbm, in_hbm,
            pl.BlockSpec((1, d), full),
        ],
        out_specs=pl.BlockSpec((FFN_TOKENS, d), tile),
        out_shape=jax.ShapeDtypeStruct((n, d), jnp.float32),
        scratch_shapes=[pltpu.VMEM((n_chunks, d, FFN_COLS), jnp.bfloat16),
                        pltpu.VMEM((n_chunks, d, FFN_COLS), jnp.bfloat16),
                        pltpu.VMEM((d_ff, d), jnp.bfloat16),
                        pltpu.VMEM((FFN_TOKENS, d), jnp.bfloat16),
                        pltpu.VMEM((FFN_TOKENS, d), jnp.float32),
                        pltpu.VMEM((W_RING, d, FFN_COLS), jnp.float32),
                        pltpu.VMEM((W_RING, d, FFN_COLS), jnp.float32),
                        pltpu.VMEM((W_RING, FFN_COLS, d), jnp.float32),
                        pltpu.SemaphoreType.DMA((3, W_RING))],
        compiler_params=pltpu.CompilerParams(
            dimension_semantics=("arbitrary",),
            vmem_limit_bytes=V7X_VMEM_LIMIT_BYTES),
        name="ffn_final" if final_norm else "ffn",
    )(x2d, g, wg, wu, wd, fg)


def kernel(x, ffn1_norm, ffn1_w_gate, ffn1_w_up, ffn1_w_down, mix_norm, w_in, gmlp_v_norm,
           gmlp_w_s, gmlp_b_s, pool_w, pool_scale, w_out, ffn2_norm, ffn2_w_gate, ffn2_w_up,
           ffn2_w_down, final_norm):
    bsz, seq, d = x.shape
    depth = ffn1_norm.shape[0]
    fg = final_norm.reshape(1, d)
    for l in range(depth):
        last = l == depth - 1
        x2d = _ffn_call(x.reshape(bsz * seq, d), ffn1_norm[l].reshape(1, d), ffn1_w_gate[l],
                        ffn1_w_up[l], ffn1_w_down[l], fg, final_norm=False)
        x3d = _mixer_call(x2d.reshape(bsz, seq, d), mix_norm[l].reshape(1, d), w_in[l],
                          gmlp_v_norm[l].reshape(1, -1), gmlp_w_s[l], gmlp_b_s[l], pool_w[l],
                          pool_scale[l].reshape(1, -1), w_out[l])
        x = _ffn_call(x3d.reshape(bsz * seq, d), ffn2_norm[l].reshape(1, d), ffn2_w_gate[l],
                      ffn2_w_up[l], ffn2_w_down[l], fg, final_norm=last).reshape(bsz, seq, d)
    return x
```

```python
import functools

import jax
import jax.numpy as jnp
from jax import lax
from jax.experimental import pallas as pl
from jax.experimental.pallas import tpu as pltpu

EPS = 1e-6
CHUNK = 128
A_HEADS = 8
A_HEAD_DIM = 64
POOL_WINDOWS = (2, 4, 8, 16)
POOL_GROUP_DIM = 128
HALO = 16
assert all(w & (w - 1) == 0 and w <= HALO for w in POOL_WINDOWS)

V7X_MXU_DIM = 256
V7X_VMEM_LIMIT_BYTES = 56 * 1024 * 1024
BF16_SUBLANES = 16

FFN_TOKENS = 1024
MIX_TOKENS = 2048
FFN_ROW_BLOCKS = (256, 256, 256, 256)
MIX_ROWS = 512
FFN_COLS = 256
M_PREP_STEPS = 2
W_RING = 4


def _rmsnorm(x, g):
    r = lax.rsqrt(jnp.mean(x * x, axis=-1, keepdims=True) + EPS)
    return (x * r) * g


def _silu(x):
    return x * (0.5 * jnp.tanh(0.5 * x) + 0.5)


def _gelu_tanh(x):
    c = 0.7978845608028654
    return 0.5 * x * (1.0 + jnp.tanh(c * (x + 0.044715 * (x * x * x))))


def _dot(a, b):
    return jnp.dot(a, b, preferred_element_type=jnp.float32)


def _store_bf16_chunk(i, src_ref, dst_ref):
    rows = src_ref.shape[0]
    r0 = pl.multiple_of(i * rows, rows)
    dst_ref[pl.ds(r0, rows), :] = src_ref[...].astype(jnp.bfloat16)


def _ffn_rows(x, g_ref, wg_s, wu_s, wd_s):
    h = _rmsnorm(x, g_ref[...]).astype(jnp.bfloat16)
    acts = []
    for c in range(wg_s.shape[0]):
        gate = _dot(h, wg_s[c])
        up = _dot(h, wu_s[c])
        acts.append((_silu(gate) * up).astype(jnp.bfloat16))
    a = jnp.concatenate(acts, axis=1)
    return x + _dot(a, wd_s[...])


def _mixer_rows(x, halo, first_pos, consts, g_ref, win_s, vg_ref, wout_s):
    seg, wcat, head_masks, bias = consts
    rows = x.shape[0]
    a_width = A_HEADS * A_HEAD_DIM
    heads_per_group = len(head_masks)

    h = _rmsnorm(x, g_ref[...]).astype(jnp.bfloat16)
    p = _dot(h, win_s[...])
    u = _gelu_tanh(p[:, :a_width]).astype(jnp.bfloat16)
    v = _gelu_tanh(p[:, a_width:2 * a_width])
    z = p[:, 2 * a_width:]

    vv = (v * v).astype(jnp.bfloat16)
    ms = jnp.concatenate(
        [_dot(vv[:, gs:gs + V7X_MXU_DIM], seg) for gs in range(0, a_width, V7X_MXU_DIM)],
        axis=1) * (1.0 / A_HEAD_DIM)
    vn = ((v * lax.rsqrt(ms + EPS)) * vg_ref[...]).astype(jnp.bfloat16)

    ya_chunks = []
    for c in range(0, rows, CHUNK):
        groups = []
        for gi in range(a_width // V7X_MXU_DIM):
            vg = vn[c:c + CHUNK, gi * V7X_MXU_DIM:(gi + 1) * V7X_MXU_DIM]
            stack = jnp.concatenate([vg * m for m in head_masks], axis=0)
            k0 = gi * heads_per_group * CHUNK
            groups.append(_dot(wcat[:, k0:k0 + heads_per_group * CHUNK], stack))
        mixed = jnp.concatenate(groups, axis=1) + bias
        ya_chunks.append(u[c:c + CHUNK, :] * mixed)
    y_a = jnp.concatenate(ya_chunks, axis=0)

    zext = jnp.concatenate([halo, z], axis=0)
    head_pos = first_pos + lax.broadcasted_iota(jnp.int32, (HALO, POOL_GROUP_DIM), 0)
    d_groups = []
    for gi, win in enumerate(POOL_WINDOWS):
        lanes = slice(gi * POOL_GROUP_DIM, (gi + 1) * POOL_GROUP_DIM)
        s = zext[:, lanes]
        step = 1
        while step < win:
            s = s + pltpu.roll(s, step, 0)
            step *= 2
        head_count = jnp.minimum(head_pos, win).astype(jnp.float32)
        mean = jnp.concatenate([s[HALO:2 * HALO, :] / head_count,
                                s[2 * HALO:, :] * (1.0 / win)], axis=0)
        d_groups.append((mean - z[:, lanes]).astype(jnp.bfloat16))

    y = jnp.concatenate([y_a.astype(jnp.bfloat16)] + d_groups, axis=1)
    return x + _dot(y, wout_s[...]), z[rows - HALO:, :]


def _prep_wout(i, n_steps, wout_ref, wp_ref, ps_ref, wout_s):
    a_width = A_HEADS * A_HEAD_DIM
    chunk_rows = wout_ref.shape[0]
    for step in range(n_steps):
        @pl.when(i == step)
        def _(step=step):
            r0 = step * chunk_rows
            if r0 < a_width:
                wout_s[r0:r0 + chunk_rows, :] = wout_ref[...].astype(jnp.bfloat16)
            else:
                for lr in range(0, chunk_rows, POOL_GROUP_DIM):
                    g = (r0 + lr - a_width) // POOL_GROUP_DIM
                    lanes = slice(g * POOL_GROUP_DIM, (g + 1) * POOL_GROUP_DIM)
                    folded = jnp.dot(wp_ref[g] * ps_ref[:, lanes], wout_ref[lr:lr + POOL_GROUP_DIM, :],
                                     precision=lax.Precision.HIGHEST,
                                     preferred_element_type=jnp.float32)
                    wout_s[r0 + lr:r0 + lr + POOL_GROUP_DIM, :] = folded.astype(jnp.bfloat16)


def _mixer_consts(ws_ref):
    row_head = lax.broadcasted_iota(jnp.int32, (V7X_MXU_DIM, V7X_MXU_DIM), 0) // A_HEAD_DIM
    col_head = lax.broadcasted_iota(jnp.int32, (V7X_MXU_DIM, V7X_MXU_DIM), 1) // A_HEAD_DIM
    seg = (row_head == col_head).astype(jnp.bfloat16)
    causal = (lax.broadcasted_iota(jnp.int32, (CHUNK, CHUNK), 1)
              <= lax.broadcasted_iota(jnp.int32, (CHUNK, CHUNK), 0))
    wcat = jnp.concatenate([jnp.where(causal, ws_ref[hh], 0.0).astype(jnp.bfloat16)
                            for hh in range(A_HEADS)], axis=1)
    lane_head = lax.broadcasted_iota(jnp.int32, (CHUNK, V7X_MXU_DIM), 1) // A_HEAD_DIM
    head_masks = [(lane_head == hh).astype(jnp.bfloat16) for hh in range(V7X_MXU_DIM // A_HEAD_DIM)]
    return seg, wcat, head_masks


def _position_bias(bs_ref):
    a_width = A_HEADS * A_HEAD_DIM
    bt = bs_ref[...].T
    bias_head = lax.broadcasted_iota(jnp.int32, (CHUNK, a_width), 1) // A_HEAD_DIM
    bias = jnp.zeros((CHUNK, a_width), jnp.float32)
    for hh in range(A_HEADS):
        bias = jnp.where(bias_head == hh, bt[:, hh:hh + 1], bias)
    return bias


def _mixer_kernel(x_ref, gm_ref, win_ref, vg_ref, ws_ref, bs_ref, wp_ref, ps_ref, wout_ref,
                  o_ref, win_s, wout_s, zprev_ref, bias_s, *, tiles_per_seq):
    i = pl.program_id(0)

    @pl.when(i == 0)
    def _():
        bias_s[...] = _position_bias(bs_ref)

    @pl.when(i < M_PREP_STEPS)
    def _():
        _store_bf16_chunk(i, win_ref, win_s)

    _prep_wout(i, M_PREP_STEPS, wout_ref, wp_ref, ps_ref, wout_s)

    @pl.when(i >= M_PREP_STEPS)
    def _():
        tm = x_ref.shape[0]
        j = (i - M_PREP_STEPS) % tiles_per_seq
        consts = _mixer_consts(ws_ref) + (bias_s[...],)

        @pl.when(j == 0)
        def _():
            zprev_ref[...] = jnp.zeros_like(zprev_ref)

        halo = zprev_ref[...]
        for r in range(0, tm, MIX_ROWS):
            x2, halo = _mixer_rows(x_ref[r:r + MIX_ROWS, :], halo, j * tm + (r + 1), consts,
                                   gm_ref, win_s, vg_ref, wout_s)
            o_ref[r:r + MIX_ROWS, :] = x2
        zprev_ref[...] = halo


def _mixer_call(x3d, gm, w_in, v_gain, w_s, b_s, w_pool, p_scale, w_out):
    b, s, d = x3d.shape
    d_in, d_mix = w_in.shape[1], w_out.shape[0]
    a_width = A_HEADS * A_HEAD_DIM
    b_width = len(POOL_WINDOWS) * POOL_GROUP_DIM
    assert s % MIX_TOKENS == 0 and MIX_TOKENS % MIX_ROWS == 0 and MIX_ROWS % CHUNK == 0
    assert d_in == 2 * a_width + b_width and d_mix == a_width + b_width
    assert d % (M_PREP_STEPS * BF16_SUBLANES) == 0 and d_mix % (M_PREP_STEPS * BF16_SUBLANES) == 0
    assert (d_mix // M_PREP_STEPS) % POOL_GROUP_DIM == 0 and a_width % (d_mix // M_PREP_STEPS) == 0
    tiles_per_seq = s // MIX_TOKENS
    full = lambda i: (0, 0)
    chunk = lambda i: (jnp.minimum(i, M_PREP_STEPS - 1), 0)

    def tile(i):
        t = jnp.maximum(i - M_PREP_STEPS, 0)
        return (t // tiles_per_seq, t % tiles_per_seq, 0)

    return pl.pallas_call(
        functools.partial(_mixer_kernel, tiles_per_seq=tiles_per_seq),
        grid=(M_PREP_STEPS + b * tiles_per_seq,),
        in_specs=[
            pl.BlockSpec((None, MIX_TOKENS, d), tile),
            pl.BlockSpec((1, d), full),
            pl.BlockSpec((d // M_PREP_STEPS, d_in), chunk),
            pl.BlockSpec((1, a_width), full),
            pl.BlockSpec(w_s.shape, lambda i: (0, 0, 0)),
            pl.BlockSpec(b_s.shape, full),
            pl.BlockSpec(w_pool.shape, lambda i: (0, 0, 0)),
            pl.BlockSpec((1, b_width), full),
            pl.BlockSpec((d_mix // M_PREP_STEPS, d), chunk),
        ],
        out_specs=pl.BlockSpec((None, MIX_TOKENS, d), tile),
        out_shape=jax.ShapeDtypeStruct((b, s, d), jnp.float32),
        scratch_shapes=[pltpu.VMEM((d, d_in), jnp.bfloat16),
                        pltpu.VMEM((d_mix, d), jnp.bfloat16),
                        pltpu.VMEM((HALO, b_width), jnp.float32),
                        pltpu.VMEM((CHUNK, a_width), jnp.float32)],
        compiler_params=pltpu.CompilerParams(
            dimension_semantics=("arbitrary",),
            vmem_limit_bytes=V7X_VMEM_LIMIT_BYTES),
        name="mixer",
    )(x3d, gm, w_in, v_gain, w_s, b_s, w_pool, p_scale, w_out)


def _ffn_kernel(x_ref, g_ref, wg_hbm, wu_hbm, wd_hbm, fg_ref, o_ref,
                wg_s, wu_s, wd_s, h_s, acc_s, wg_buf, wu_buf, wd_buf, sem, *, final_norm):
    i = pl.program_id(0)
    n_chunks = wg_s.shape[0]

    def chunk_copies(c, slot):
        cols = pl.ds(pl.multiple_of(c * FFN_COLS, FFN_COLS), FFN_COLS)
        return (pltpu.make_async_copy(wg_hbm.at[:, cols], wg_buf.at[slot], sem.at[0, slot]),
                pltpu.make_async_copy(wu_hbm.at[:, cols], wu_buf.at[slot], sem.at[1, slot]),
                pltpu.make_async_copy(wd_hbm.at[cols, :], wd_buf.at[slot], sem.at[2, slot]))

    @pl.when(i == 0)
    def _():
        for c in range(W_RING):
            for cp in chunk_copies(c, c):
                cp.start()
        h_s[...] = _rmsnorm(x_ref[...], g_ref[...]).astype(jnp.bfloat16)
        acc_s[...] = jnp.zeros_like(acc_s)

    @pl.when(i < n_chunks)
    def _():
        slot = i % W_RING
        for cp in chunk_copies(i, slot):
            cp.wait()
        wg = wg_buf[slot].astype(jnp.bfloat16)
        wu = wu_buf[slot].astype(jnp.bfloat16)
        wd = (0.5 * wd_buf[slot]).astype(jnp.bfloat16)

        @pl.when(i + W_RING < n_chunks)
        def _():
            for cp in chunk_copies(i + W_RING, slot):
                cp.start()

        wg_s[i] = wg
        wu_s[i] = wu
        wd_s[pl.ds(pl.multiple_of(i * FFN_COLS, FFN_COLS), FFN_COLS), :] = wd
        h = h_s[...]
        a = (_silu(_dot(h, wg)) * _dot(h, wu)).astype(jnp.bfloat16)
        acc_s[...] += _dot(a, wd)

    @pl.when(i == n_chunks - 1)
    def _():
        y = x_ref[...] + acc_s[...]
        if final_norm:
            y = _rmsnorm(y, fg_ref[...])
        o_ref[...] = y

    @pl.when(i >= n_chunks)
    def _():
        r = 0
        for rows in FFN_ROW_BLOCKS:
            y = _ffn_rows(x_ref[r:r + rows, :], g_ref, wg_s, wu_s, wd_s)
            if final_norm:
                y = _rmsnorm(y, fg_ref[...])
            o_ref[r:r + rows, :] = y
            r += rows


def _ffn_call(x2d, g, wg, wu, wd, fg, *, final_norm):
    n, d = x2d.shape
    d_ff = wg.shape[1]
    assert n % FFN_TOKENS == 0 and sum(FFN_ROW_BLOCKS) == FFN_TOKENS and d_ff % FFN_COLS == 0
    n_chunks = d_ff // FFN_COLS
    assert n_chunks >= W_RING
    full = lambda i: (0, 0)
    tile = lambda i: (jnp.maximum(i - (n_chunks - 1), 0), 0)
    in_hbm = pl.BlockSpec(memory_space=pl.ANY)
    return pl.pallas_call(
        functools.partial(_ffn_kernel, final_norm=final_norm),
        grid=(n_chunks - 1 + n // FFN_TOKENS,),
        in_specs=[
            pl.BlockSpec((FFN_TOKENS, d), tile),
            pl.BlockSpec((1, d), full),
            in_hbm, in_hbm, in_hbm,
            pl.BlockSpec((1, d), full),
        ],
        out_specs=pl.BlockSpec((FFN_TOKENS, d), tile),
        out_shape=jax.ShapeDtypeStruct((n, d), jnp.float32),
        scratch_shapes=[pltpu.VMEM((n_chunks, d, FFN_COLS), jnp.bfloat16),
                        pltpu.VMEM((n_chunks, d, FFN_COLS), jnp.bfloat16),
                        pltpu.VMEM((d_ff, d), jnp.bfloat16),
                        pltpu.VMEM((FFN_TOKENS, d), jnp.bfloat16),
                        pltpu.VMEM((FFN_TOKENS, d), jnp.float32),
                        pltpu.VMEM((W_RING, d, FFN_COLS), jnp.float32),
                        pltpu.VMEM((W_RING, d, FFN_COLS), jnp.float32),
                        pltpu.VMEM((W_RING, FFN_COLS, d), jnp.float32),
                        pltpu.SemaphoreType.DMA((3, W_RING))],
        compiler_params=pltpu.CompilerParams(
            dimension_semantics=("arbitrary",),
            vmem_limit_bytes=V7X_VMEM_LIMIT_BYTES),
        name="ffn_final" if final_norm else "ffn",
    )(x2d, g, wg, wu, wd, fg)


def kernel(x, ffn1_norm, ffn1_w_gate, ffn1_w_up, ffn1_w_down, mix_norm, w_in, gmlp_v_norm,
           gmlp_w_s, gmlp_b_s, pool_w, pool_scale, w_out, ffn2_norm, ffn2_w_gate, ffn2_w_up,
           ffn2_w_down, final_norm):
    bsz, seq, d = x.shape
    depth = ffn1_norm.shape[0]
    fg = final_norm.reshape(1, d)
    for l in range(depth):
        last = l == depth - 1
        x2d = _ffn_call(x.reshape(bsz * seq, d), ffn1_norm[l].reshape(1, d), ffn1_w_gate[l],
                        ffn1_w_up[l], ffn1_w_down[l], fg, final_norm=False)
        x3d = _mixer_call(x2d.reshape(bsz, seq, d), mix_norm[l].reshape(1, d), w_in[l],
                          gmlp_v_norm[l].reshape(1, -1), gmlp_w_s[l], gmlp_b_s[l], pool_w[l],
                          pool_scale[l].reshape(1, -1), w_out[l])
        x = _ffn_call(x3d.reshape(bsz * seq, d), ffn2_norm[l].reshape(1, d), ffn2_w_gate[l],
                      ffn2_w_up[l], ffn2_w_down[l], fg, final_norm=last).reshape(bsz, seq, d)
    return x
```

```python
import functools

import jax
import jax.numpy as jnp
from jax import lax
from jax.experimental import pallas as pl
from jax.experimental.pallas import tpu as pltpu

EPS = 1e-6
CHUNK = 128
A_HEADS = 8
A_HEAD_DIM = 64
POOL_WINDOWS = (2, 4, 8, 16)
POOL_GROUP_DIM = 128
HALO = 16
assert all(w & (w - 1) == 0 and w <= HALO for w in POOL_WINDOWS)

V7X_MXU_DIM = 256
V7X_VMEM_LIMIT_BYTES = 56 * 1024 * 1024
BF16_SUBLANES = 16

FFN_TOKENS = 1024
MIX_TOKENS = 2048
FFN_ROW_BLOCKS = (256, 256, 256, 256)
MIX_ROWS = 512
FFN_COLS = 256
M_PREP_STEPS = 2
W_RING = 4


def _rmsnorm(x, g):
    r = lax.rsqrt(jnp.mean(x * x, axis=-1, keepdims=True) + EPS)
    return (x * r) * g


def _silu(x):
    return x * (0.5 * jnp.tanh(0.5 * x) + 0.5)


def _gelu_tanh(x):
    c = 0.7978845608028654
    return 0.5 * x * (1.0 + jnp.tanh(c * (x + 0.044715 * (x * x * x))))


def _dot(a, b):
    return jnp.dot(a, b, preferred_element_type=jnp.float32)


def _store_bf16_chunk(i, src_ref, dst_ref):
    rows = src_ref.shape[0]
    r0 = pl.multiple_of(i * rows, rows)
    dst_ref[pl.ds(r0, rows), :] = src_ref[...].astype(jnp.bfloat16)


def _ffn_rows(x, g_ref, wg_s, wu_s, wd_s):
    h = _rmsnorm(x, g_ref[...]).astype(jnp.bfloat16)
    acts = []
    for c in range(wg_s.shape[0]):
        gate = _dot(h, wg_s[c])
        up = _dot(h, wu_s[c])
        acts.append((_silu(gate) * up).astype(jnp.bfloat16))
    a = jnp.concatenate(acts, axis=1)
    return x + _dot(a, wd_s[...])


def _mixer_rows(x, halo, first_pos, consts, g_ref, win_s, vg_ref, wout_s):
    seg, wcat, head_masks, bias = consts
    rows = x.shape[0]
    a_width = A_HEADS * A_HEAD_DIM
    heads_per_group = len(head_masks)

    h = _rmsnorm(x, g_ref[...]).astype(jnp.bfloat16)
    p = _dot(h, win_s[...])
    u = _gelu_tanh(p[:, :a_width])
    v = _gelu_tanh(p[:, a_width:2 * a_width])
    z = p[:, 2 * a_width:]

    vv = (v * v).astype(jnp.bfloat16)
    ms = jnp.concatenate(
        [_dot(vv[:, gs:gs + V7X_MXU_DIM], seg) for gs in range(0, a_width, V7X_MXU_DIM)],
        axis=1) * (1.0 / A_HEAD_DIM)
    vn = ((v * lax.rsqrt(ms + EPS)) * vg_ref[...]).astype(jnp.bfloat16)

    ya_chunks = []
    for c in range(0, rows, CHUNK):
        groups = []
        for gi in range(a_width // V7X_MXU_DIM):
            vg = vn[c:c + CHUNK, gi * V7X_MXU_DIM:(gi + 1) * V7X_MXU_DIM]
            stack = jnp.concatenate([vg * m for m in head_masks], axis=0)
            k0 = gi * heads_per_group * CHUNK
            groups.append(_dot(wcat[:, k0:k0 + heads_per_group * CHUNK], stack))
        mixed = jnp.concatenate(groups, axis=1) + bias
        ya_chunks.append(u[c:c + CHUNK, :] * mixed)
    y_a = jnp.concatenate(ya_chunks, axis=0)

    zext = jnp.concatenate([halo, z], axis=0)
    head_pos = first_pos + lax.broadcasted_iota(jnp.int32, (HALO, POOL_GROUP_DIM), 0)
    d_groups = []
    for gi, win in enumerate(POOL_WINDOWS):
        lanes = slice(gi * POOL_GROUP_DIM, (gi + 1) * POOL_GROUP_DIM)
        s = zext[:, lanes]
        step = 1
        while step < win:
            s = s + pltpu.roll(s, step, 0)
            step *= 2
        head_count = jnp.minimum(head_pos, win).astype(jnp.float32)
        mean = jnp.concatenate([s[HALO:2 * HALO, :] / head_count,
                                s[2 * HALO:, :] * (1.0 / win)], axis=0)
        d_groups.append((mean - z[:, lanes]).astype(jnp.bfloat16))

    y = jnp.concatenate([y_a.astype(jnp.bfloat16)] + d_groups, axis=1)
    return x + _dot(y, wout_s[...]), z[rows - HALO:, :]


def _prep_wout(i, n_steps, wout_ref, wp_ref, ps_ref, wout_s):
    a_width = A_HEADS * A_HEAD_DIM
    chunk_rows = wout_ref.shape[0]
    for step in range(n_steps):
        @pl.when(i == step)
        def _(step=step):
            r0 = step * chunk_rows
            if r0 < a_width:
                wout_s[r0:r0 + chunk_rows, :] = wout_ref[...].astype(jnp.bfloat16)
            else:
                for lr in range(0, chunk_rows, POOL_GROUP_DIM):
                    g = (r0 + lr - a_width) // POOL_GROUP_DIM
                    lanes = slice(g * POOL_GROUP_DIM, (g + 1) * POOL_GROUP_DIM)
                    folded = jnp.dot(wp_ref[g] * ps_ref[:, lanes], wout_ref[lr:lr + POOL_GROUP_DIM, :],
                                     precision=lax.Precision.HIGHEST,
                                     preferred_element_type=jnp.float32)
                    wout_s[r0 + lr:r0 + lr + POOL_GROUP_DIM, :] = folded.astype(jnp.bfloat16)


def _mixer_consts(ws_ref):
    row_head = lax.broadcasted_iota(jnp.int32, (V7X_MXU_DIM, V7X_MXU_DIM), 0) // A_HEAD_DIM
    col_head = lax.broadcasted_iota(jnp.int32, (V7X_MXU_DIM, V7X_MXU_DIM), 1) // A_HEAD_DIM
    seg = (row_head == col_head).astype(jnp.bfloat16)
    causal = (lax.broadcasted_iota(jnp.int32, (CHUNK, CHUNK), 1)
              <= lax.broadcasted_iota(jnp.int32, (CHUNK, CHUNK), 0))
    wcat = jnp.concatenate([jnp.where(causal, ws_ref[hh], 0.0).astype(jnp.bfloat16)
                            for hh in range(A_HEADS)], axis=1)
    lane_head = lax.broadcasted_iota(jnp.int32, (CHUNK, V7X_MXU_DIM), 1) // A_HEAD_DIM
    head_masks = [(lane_head == hh).astype(jnp.bfloat16) for hh in range(V7X_MXU_DIM // A_HEAD_DIM)]
    return seg, wcat, head_masks


def _position_bias(bs_ref):
    a_width = A_HEADS * A_HEAD_DIM
    bt = bs_ref[...].T
    bias_head = lax.broadcasted_iota(jnp.int32, (CHUNK, a_width), 1) // A_HEAD_DIM
    bias = jnp.zeros((CHUNK, a_width), jnp.float32)
    for hh in range(A_HEADS):
        bias = jnp.where(bias_head == hh, bt[:, hh:hh + 1], bias)
    return bias


def _mixer_kernel(x_ref, gm_ref, win_ref, vg_ref, ws_ref, bs_ref, wp_ref, ps_ref, wout_ref,
                  o_ref, win_s, wout_s, zprev_ref, bias_s, *, tiles_per_seq):
    i = pl.program_id(0)

    @pl.when(i == 0)
    def _():
        bias_s[...] = _position_bias(bs_ref)

    @pl.when(i < M_PREP_STEPS)
    def _():
        _store_bf16_chunk(i, win_ref, win_s)

    _prep_wout(i, M_PREP_STEPS, wout_ref, wp_ref, ps_ref, wout_s)

    @pl.when(i >= M_PREP_STEPS)
    def _():
        tm = x_ref.shape[0]
        j = (i - M_PREP_STEPS) % tiles_per_seq
        consts = _mixer_consts(ws_ref) + (bias_s[...],)

        @pl.when(j == 0)
        def _():
            zprev_ref[...] = jnp.zeros_like(zprev_ref)

        halo = zprev_ref[...]
        for r in range(0, tm, MIX_ROWS):
            x2, halo = _mixer_rows(x_ref[r:r + MIX_ROWS, :], halo, j * tm + (r + 1), consts,
                                   gm_ref, win_s, vg_ref, wout_s)
            o_ref[r:r + MIX_ROWS, :] = x2
        zprev_ref[...] = halo


def _mixer_call(x3d, gm, w_in, v_gain, w_s, b_s, w_pool, p_scale, w_out):
    b, s, d = x3d.shape
    d_in, d_mix = w_in.shape[1], w_out.shape[0]
    a_width = A_HEADS * A_HEAD_DIM
    b_width = len(POOL_WINDOWS) * POOL_GROUP_DIM
    assert s % MIX_TOKENS == 0 and MIX_TOKENS % MIX_ROWS == 0 and MIX_ROWS % CHUNK == 0
    assert d_in == 2 * a_width + b_width and d_mix == a_width + b_width
    assert d % (M_PREP_STEPS * BF16_SUBLANES) == 0 and d_mix % (M_PREP_STEPS * BF16_SUBLANES) == 0
    assert (d_mix // M_PREP_STEPS) % POOL_GROUP_DIM == 0 and a_width % (d_mix // M_PREP_STEPS) == 0
    tiles_per_seq = s // MIX_TOKENS
    full = lambda i: (0, 0)
    chunk = lambda i: (jnp.minimum(i, M_PREP_STEPS - 1), 0)

    def tile(i):
        t = jnp.maximum(i - M_PREP_STEPS, 0)
        return (t // tiles_per_seq, t % tiles_per_seq, 0)

    return pl.pallas_call(
        functools.partial(_mixer_kernel, tiles_per_seq=tiles_per_seq),
        grid=(M_PREP_STEPS + b * tiles_per_seq,),
        in_specs=[
            pl.BlockSpec((None, MIX_TOKENS, d), tile),
            pl.BlockSpec((1, d), full),
            pl.BlockSpec((d // M_PREP_STEPS, d_in), chunk),
            pl.BlockSpec((1, a_width), full),
            pl.BlockSpec(w_s.shape, lambda i: (0, 0, 0)),
            pl.BlockSpec(b_s.shape, full),
            pl.BlockSpec(w_pool.shape, lambda i: (0, 0, 0)),
            pl.BlockSpec((1, b_width), full),
            pl.BlockSpec((d_mix // M_PREP_STEPS, d), chunk),
        ],
        out_specs=pl.BlockSpec((None, MIX_TOKENS, d), tile),
        out_shape=jax.ShapeDtypeStruct((b, s, d), jnp.float32),
        scratch_shapes=[pltpu.VMEM((d, d_in), jnp.bfloat16),
                        pltpu.VMEM((d_mix, d), jnp.bfloat16),
                        pltpu.VMEM((HALO, b_width), jnp.float32),
                        pltpu.VMEM((CHUNK, a_width), jnp.float32)],
        compiler_params=pltpu.CompilerParams(
            dimension_semantics=("arbitrary",),
            vmem_limit_bytes=V7X_VMEM_LIMIT_BYTES),
        name="mixer",
    )(x3d, gm, w_in, v_gain, w_s, b_s, w_pool, p_scale, w_out)


def _ffn_kernel(x_ref, g_ref, wg_hbm, wu_hbm, wd_hbm, fg_ref, o_ref,
                wg_s, wu_s, wd_s, h_s, acc_s, wg_buf, wu_buf, wd_buf, sem, *, final_norm):
    i = pl.program_id(0)
    n_chunks = wg_s.shape[0]

    def chunk_copies(c, slot):
        cols = pl.ds(pl.multiple_of(c * FFN_COLS, FFN_COLS), FFN_COLS)
        return (pltpu.make_async_copy(wg_hbm.at[:, cols], wg_buf.at[slot], sem.at[0, slot]),
                pltpu.make_async_copy(wu_hbm.at[:, cols], wu_buf.at[slot], sem.at[1, slot]),
                pltpu.make_async_copy(wd_hbm.at[cols, :], wd_buf.at[slot], sem.at[2, slot]))

    @pl.when(i == 0)
    def _():
        for c in range(W_RING):
            for cp in chunk_copies(c, c):
                cp.start()
        h_s[...] = _rmsnorm(x_ref[...], g_ref[...]).astype(jnp.bfloat16)
        acc_s[...] = jnp.zeros_like(acc_s)

    @pl.when(i < n_chunks)
    def _():
        slot = i % W_RING
        for cp in chunk_copies(i, slot):
            cp.wait()
        wg = wg_buf[slot].astype(jnp.bfloat16)
        wu = wu_buf[slot].astype(jnp.bfloat16)
        wd = (0.5 * wd_buf[slot]).astype(jnp.bfloat16)
        wg_s[i] = wg
        wu_s[i] = wu
        wd_s[pl.ds(pl.multiple_of(i * FFN_COLS, FFN_COLS), FFN_COLS), :] = wd
        h = h_s[...]
        a = (_silu(_dot(h, wg)) * _dot(h, wu)).astype(jnp.bfloat16)
        acc_s[...] += _dot(a, wd)

        @pl.when(i + W_RING < n_chunks)
        def _():
            for cp in chunk_copies(i + W_RING, slot):
                cp.start()

    @pl.when(i == n_chunks - 1)
    def _():
        y = x_ref[...] + acc_s[...]
        if final_norm:
            y = _rmsnorm(y, fg_ref[...])
        o_ref[...] = y

    @pl.when(i >= n_chunks)
    def _():
        r = 0
        for rows in FFN_ROW_BLOCKS:
            y = _ffn_rows(x_ref[r:r + rows, :], g_ref, wg_s, wu_s, wd_s)
            if final_norm:
                y = _rmsnorm(y, fg_ref[...])
            o_ref[r:r + rows, :] = y
            r += rows


def _ffn_call(x2d, g, wg, wu, wd, fg, *, final_norm):
    n, d = x2d.shape
    d_ff = wg.shape[1]
    assert n % FFN_TOKENS == 0 and sum(FFN_ROW_BLOCKS) == FFN_TOKENS and d_ff % FFN_COLS == 0
    n_chunks = d_ff // FFN_COLS
    assert n_chunks >= W_RING
    full = lambda i: (0, 0)
    tile = lambda i: (jnp.maximum(i - (n_chunks - 1), 0), 0)
    in_hbm = pl.BlockSpec(memory_space=pl.ANY)
    return pl.pallas_call(
        functools.partial(_ffn_kernel, final_norm=final_norm),
        grid=(n_chunks - 1 + n // FFN_TOKENS,),
        in_specs=[
            pl.BlockSpec((FFN_TOKENS, d), tile),
            pl.BlockSpec((1, d), full),
            in_hbm, in_hbm, in_hbm,
            pl.BlockSpec((1, d), full),
        ],
        out_specs=pl.BlockSpec((FFN_TOKENS, d), tile),
        out_shape=jax.ShapeDtypeStruct((n, d), jnp.float32),
        scratch_shapes=[pltpu.VMEM((n_chunks, d, FFN_COLS), jnp.bfloat16),
                        pltpu.VMEM((n_chunks, d, FFN_COLS), jnp.bfloat16),
                        pltpu.VMEM((d_ff, d), jnp.bfloat16),
                        pltpu.VMEM((FFN_TOKENS, d), jnp.bfloat16),
                        pltpu.VMEM((FFN_TOKENS, d), jnp.float32),
                        pltpu.VMEM((W_RING, d, FFN_COLS), jnp.float32),
                        pltpu.VMEM((W_RING, d, FFN_COLS), jnp.float32),
                        pltpu.VMEM((W_RING, FFN_COLS, d), jnp.float32),
                        pltpu.SemaphoreType.DMA((3, W_RING))],
        compiler_params=pltpu.CompilerParams(
            dimension_semantics=("arbitrary",),
            vmem_limit_bytes=V7X_VMEM_LIMIT_BYTES),
        name="ffn_final" if final_norm else "ffn",
    )(x2d, g, wg, wu, wd, fg)


def kernel(x, ffn1_norm, ffn1_w_gate, ffn1_w_up, ffn1_w_down, mix_norm, w_in, gmlp_v_norm,
           gmlp_w_s, gmlp_b_s, pool_w, pool_scale, w_out, ffn2_norm, ffn2_w_gate, ffn2_w_up,
           ffn2_w_down, final_norm):
    bsz, seq, d = x.shape
    depth = ffn1_norm.shape[0]
    fg = final_norm.reshape(1, d)
    for l in range(depth):
        last = l == depth - 1
        x2d = _ffn_call(x.reshape(bsz * seq, d), ffn1_norm[l].reshape(1, d), ffn1_w_gate[l],
                        ffn1_w_up[l], ffn1_w_down[l], fg, final_norm=False)
        x3d = _mixer_call(x2d.reshape(bsz, seq, d), mix_norm[l].reshape(1, d), w_in[l],
                          gmlp_v_norm[l].reshape(1, -1), gmlp_w_s[l], gmlp_b_s[l], pool_w[l],
                          pool_scale[l].reshape(1, -1), w_out[l])
        x = _ffn_call(x3d.reshape(bsz * seq, d), ffn2_norm[l].reshape(1, d), ffn2_w_gate[l],
                      ffn2_w_up[l], ffn2_w_down[l], fg, final_norm=last).reshape(bsz, seq, d)
    return x
```
